```python
import math
import jax, jax.numpy as jnp
from jax import lax
import numpy as np

D_MODEL = 1024
BATCH = 8
SEQ = 8192
DEPTH = 1

EXPAND = 2
D_INNER = EXPAND * D_MODEL
D_SSM = D_INNER // 2
D_ATTN = D_INNER - D_SSM
SSM_HEAD_DIM = 64
SSM_HEADS = D_SSM // SSM_HEAD_DIM
SSM_GROUPS = 2
SSM_HPG = SSM_HEADS // SSM_GROUPS
SSM_STATE = 128
CONV_K = 4
CHUNK = 128
CONV_DIM = D_SSM + 2 * SSM_GROUPS * SSM_STATE
ATTN_HEAD_DIM = 64
ATTN_HEADS = D_ATTN // ATTN_HEAD_DIM
Q_BLOCK = 128
EPS = 1e-6

N_IN = (2 * D_SSM + 2 * SSM_GROUPS * SSM_STATE + SSM_HEADS) + (4 * D_ATTN + ATTN_HEADS)
SPLITS = np.cumsum([D_SSM, D_SSM, SSM_GROUPS * SSM_STATE, SSM_GROUPS * SSM_STATE, SSM_HEADS,
                    D_ATTN, D_ATTN, D_ATTN, D_ATTN]).tolist()

kernel_name = "hymba_ssd_fox_adaln_block"


def rms_norm(x, gain):
    xf = x.astype(jnp.float32)
    out = xf * lax.rsqrt(jnp.mean(xf * xf, axis=-1, keepdims=True) + EPS)
    return (out * gain.astype(jnp.float32)).astype(x.dtype)


def causal_depthwise_conv(u, w, b):
    out = lax.conv_general_dilated(u, w[:, None, :].astype(u.dtype), window_strides=(1,),
                                   padding=[(CONV_K - 1, 0)],
                                   dimension_numbers=("NWC", "WIO", "NWC"),
                                   feature_group_count=u.shape[-1])
    return out + b.astype(u.dtype)


def ssd_chunked(xs, dt, a, bm, cm):
    bsz, seq = xs.shape[:2]
    nc = seq // CHUNK
    xs = xs.reshape(bsz, nc, CHUNK, SSM_GROUPS, SSM_HPG, SSM_HEAD_DIM).astype(jnp.float32)
    dt = dt.reshape(bsz, nc, CHUNK, SSM_GROUPS, SSM_HPG)
    bm = bm.reshape(bsz, nc, CHUNK, SSM_GROUPS, SSM_STATE).astype(jnp.float32)
    cm = cm.reshape(bsz, nc, CHUNK, SSM_GROUPS, SSM_STATE).astype(jnp.float32)
    cum = jnp.cumsum(dt * a.reshape(SSM_GROUPS, SSM_HPG), axis=2)
    xdt = xs * dt[..., None]
    causal = jnp.tril(jnp.ones((CHUNK, CHUNK), dtype=bool))
    seg = cum[:, :, :, None] - cum[:, :, None, :]
    decay_in = jnp.exp(jnp.where(causal[None, None, :, :, None, None], seg, -jnp.inf))
    cb = jnp.einsum('bclgn,bcsgn->bclsg', cm, bm)
    y_diag = jnp.einsum('bclsgr,bcsgrp->bclgrp', cb[..., None] * decay_in, xdt)
    decay_to_end = jnp.exp(cum[:, :, -1:] - cum)
    states = jnp.einsum('bclgn,bclgr,bclgrp->bcgrpn', bm, decay_to_end, xdt)
    chunk_decay = jnp.exp(cum[:, :, -1])

    def step(h, inp):
        st, dec = inp
        return h * dec[..., None, None] + st, h

    h0 = jnp.zeros((bsz, SSM_GROUPS, SSM_HPG, SSM_HEAD_DIM, SSM_STATE), jnp.float32)
    _, prev = lax.scan(step, h0, (jnp.moveaxis(states, 1, 0), jnp.moveaxis(chunk_decay, 1, 0)))
    prev = jnp.moveaxis(prev, 0, 1)
    y_off = jnp.einsum('bclgn,bcgrpn,bclgr->bclgrp', cm, prev, jnp.exp(cum))
    return (y_diag + y_off).reshape(bsz, seq, SSM_HEADS, SSM_HEAD_DIM)


def forgetting_attention(q, k, v, log_f):
    seq = q.shape[1]
    scale = 1.0 / math.sqrt(ATTN_HEAD_DIM)
    fcum = jnp.transpose(jnp.cumsum(log_f, axis=1), (0, 2, 1))
    outs = []
    for i in range(seq // Q_BLOCK):
        qs, qe = i * Q_BLOCK, (i + 1) * Q_BLOCK
        s = jnp.einsum('bqhd,bkhd->bhqk', q[:, qs:qe], k[:, :qe]).astype(jnp.float32) * scale
        s = s + fcum[:, :, qs:qe, None] - fcum[:, :, None, :qe]
        mask = jnp.arange(qe)[None, :] <= (qs + jnp.arange(Q_BLOCK))[:, None]
        s = jnp.where(mask, s, -jnp.inf)
        p = jax.nn.softmax(s, axis=-1).astype(v.dtype)
        outs.append(jnp.einsum('bhqk,bkhd->bqhd', p, v[:, :qe]))
    return jnp.concatenate(outs, axis=1)


def hybrid_layer(x, c, norm_gain, w_ada, b_ada, w_in, conv_w, conv_b, dt_bias, a_log, d_skip,
                 ssm_norm_gain, q_norm_gain, k_norm_gain, forget_bias, attn_norm_gain, w_out):
    bsz, seq, _ = x.shape
    mod = jax.nn.silu(c) @ w_ada + b_ada
    shift, scale, gate = jnp.split(mod, 3, axis=-1)
    h = rms_norm(x, norm_gain) * (1 + scale[:, None, :]) + shift[:, None, :]

    proj = h @ w_in
    z_ssd, x_ssd, b_ssd, c_ssd, dt_raw, q, k, v, z_attn, f_raw = jnp.split(proj, SPLITS, axis=-1)

    xbc = jax.nn.silu(causal_depthwise_conv(jnp.concatenate([x_ssd, b_ssd, c_ssd], -1), conv_w, conv_b))
    x_c, b_c, c_c = jnp.split(xbc, [D_SSM, D_SSM + SSM_GROUPS * SSM_STATE], axis=-1)
    dt = jax.nn.softplus(dt_raw.astype(jnp.float32) + dt_bias.astype(jnp.float32))
    a = -jnp.exp(a_log.astype(jnp.float32))
    xh = x_c.reshape(bsz, seq, SSM_HEADS, SSM_HEAD_DIM)
    y = ssd_chunked(xh, dt, a,
                    b_c.reshape(bsz, seq, SSM_GROUPS, SSM_STATE),
                    c_c.reshape(bsz, seq, SSM_GROUPS, SSM_STATE))
    y = (y + xh.astype(jnp.float32) * d_skip.astype(jnp.float32)[:, None]).astype(x.dtype)
    y = y.reshape(bsz, seq, D_SSM) * jax.nn.silu(z_ssd)
    y = rms_norm(y.reshape(bsz, seq, SSM_GROUPS, D_SSM // SSM_GROUPS),
                 ssm_norm_gain.reshape(SSM_GROUPS, -1)).reshape(bsz, seq, D_SSM)

    hs = (bsz, seq, ATTN_HEADS, ATTN_HEAD_DIM)
    qh = rms_norm(q.reshape(hs), q_norm_gain)
    kh = rms_norm(k.reshape(hs), k_norm_gain)
    log_f = jax.nn.log_sigmoid(f_raw.astype(jnp.float32) + forget_bias.astype(jnp.float32))
    o = forgetting_attention(qh, kh, v.reshape(hs), log_f)
    o = rms_norm(o, attn_norm_gain.reshape(ATTN_HEADS, ATTN_HEAD_DIM)).reshape(bsz, seq, D_ATTN)
    o = o * jax.nn.silu(z_attn)

    mixed = jnp.concatenate([y, o], axis=-1) @ w_out
    return x + gate[:, None, :] * mixed


def setup_inputs(seed: int = 0) -> dict:
    key = jax.random.key(seed)
    ks = jax.random.split(key, 20)
    f32 = jnp.float32
    nrm = lambda k, shape, s: jax.random.normal(k, shape, f32) * s
    dt0 = jnp.exp(jax.random.uniform(ks[8], (DEPTH, SSM_HEADS), f32, math.log(1e-3), math.log(1e-1)))
    return {
        "x": jax.random.normal(ks[0], (BATCH, SEQ, D_MODEL), f32),
        "c": jax.random.normal(ks[1], (BATCH, D_MODEL), f32),
        "norm_gain": 1.0 + nrm(ks[2], (DEPTH, D_MODEL), 0.02),
        "w_ada": nrm(ks[3], (DEPTH, D_MODEL, 3 * D_MODEL), D_MODEL ** -0.5 * 0.5),
        "b_ada": nrm(ks[4], (DEPTH, 3 * D_MODEL), 0.02),
        "w_in": nrm(ks[5], (DEPTH, D_MODEL, N_IN), D_MODEL ** -0.5),
        "conv_w": nrm(ks[6], (DEPTH, CONV_K, CONV_DIM), CONV_K ** -0.5),
        "conv_b": nrm(ks[7], (DEPTH, CONV_DIM), 0.02),
        "dt_bias": dt0 + jnp.log(-jnp.expm1(-dt0)),
        "a_log": jnp.log(jax.random.uniform(ks[9], (DEPTH, SSM_HEADS), f32, 1.0, 16.0)),
        "d_skip": 1.0 + nrm(ks[10], (DEPTH, SSM_HEADS), 0.02),
        "ssm_norm_gain": 1.0 + nrm(ks[11], (DEPTH, D_SSM), 0.02),
        "q_norm_gain": 1.0 + nrm(ks[12], (DEPTH, ATTN_HEAD_DIM), 0.02),
        "k_norm_gain": 1.0 + nrm(ks[13], (DEPTH, ATTN_HEAD_DIM), 0.02),
        "forget_bias": jax.random.uniform(ks[14], (DEPTH, ATTN_HEADS), f32, 1.0, 5.0),
        "attn_norm_gain": 1.0 + nrm(ks[15], (DEPTH, D_ATTN), 0.02),
        "w_out": nrm(ks[16], (DEPTH, D_INNER, D_MODEL), D_INNER ** -0.5),
    }


def reference(x, c, norm_gain, w_ada, b_ada, w_in, conv_w, conv_b, dt_bias, a_log, d_skip,
              ssm_norm_gain, q_norm_gain, k_norm_gain, forget_bias, attn_norm_gain, w_out):
    for layer in range(DEPTH):
        x = hybrid_layer(x, c, norm_gain[layer], w_ada[layer], b_ada[layer], w_in[layer],
                         conv_w[layer], conv_b[layer], dt_bias[layer], a_log[layer], d_skip[layer],
                         ssm_norm_gain[layer], q_norm_gain[layer], k_norm_gain[layer],
                         forget_bias[layer], attn_norm_gain[layer], w_out[layer])
    return x
```

```python
import functools
import math

import jax
import jax.numpy as jnp
from jax import lax
from jax.experimental import pallas as pl
from jax.experimental.pallas import tpu as pltpu

F32 = jnp.float32
BF16 = jnp.bfloat16

D_MODEL = 1024
D_SSM = 1024
D_ATTN = 1024
SSM_HEAD_DIM = 64
SSM_HEADS = 16
SSM_GROUPS = 2
SSM_STATE = 128
CONV_K = 4
CHUNK = 128
D_BC = SSM_GROUPS * SSM_STATE
CONV_DIM = D_SSM + 2 * D_BC
ATTN_HEAD_DIM = 64
ATTN_HEADS = 16
EPS = 1e-6
LANES = 128
HEAD_PAIRS = ATTN_HEADS // 2
GATE_COLS = LANES

TM_PROJ = 512
TG = 512
TQ = 256
K_PREP_ROWS = 1024
NEG_BIG = -1e30
VMEM_LIMIT = 56 * 1024 * 1024


def _silu(v):
    return v * jax.nn.sigmoid(v)


def _const_spec(shape):
    nd = len(shape)
    return pl.BlockSpec(shape, lambda *_: (0,) * nd, pipeline_mode=pl.Buffered(1))


def _ada_kernel(c_ref, w_ref, b_ref, o_ref):
    c = c_ref[...]
    o_ref[0] = jnp.dot(_silu(c), w_ref[...], precision=lax.Precision.HIGHEST,
                       preferred_element_type=F32) + b_ref[0]


def _ada_call(c, w_ada, b_ada):
    bsz = c.shape[0]
    return pl.pallas_call(
        _ada_kernel,
        grid=(3,),
        in_specs=[pl.BlockSpec((bsz, D_MODEL), lambda j: (0, 0)),
                  pl.BlockSpec((D_MODEL, D_MODEL), lambda j: (0, j)),
                  pl.BlockSpec((1, 1, D_MODEL), lambda j: (j, 0, 0))],
        out_specs=pl.BlockSpec((1, bsz, D_MODEL), lambda j: (j, 0, 0)),
        out_shape=jax.ShapeDtypeStruct((3, bsz, D_MODEL), F32),
        name="adaln_mod",
    )(c, w_ada, b_ada.reshape(3, 1, D_MODEL))


def _inproj_kernel(x_ref, mod_ref, gain_ref, wz, wxbc, wg, wq, wk, wv, wza,
                   oz, oxbc, og, oq, ok, ov, oza):
    x = x_ref[...]
    ms = jnp.mean(x * x, axis=-1, keepdims=True)
    shift = mod_ref[0, 0]
    scale = mod_ref[1, 0]
    h = (x * lax.rsqrt(ms + EPS) * gain_ref[...]) * (1.0 + scale) + shift
    hb = h.astype(BF16)
    for w_ref, o_ref in ((wz, oz), (wxbc, oxbc), (wg, og), (wq, oq), (wk, ok), (wv, ov), (wza, oza)):
        o_ref[...] = jnp.dot(hb, w_ref[...], preferred_element_type=F32).astype(o_ref.dtype)


def _inproj_call(x2, mod4, gain, weights, seq):
    rows = x2.shape[0]
    tiles_per_batch = seq // TM_PROJ
    widths = [w.shape[1] for w in weights]
    dtypes = [BF16, BF16, F32, BF16, BF16, BF16, BF16]
    row_spec = lambda n: pl.BlockSpec((TM_PROJ, n), lambda i: (i, 0))
    return pl.pallas_call(
        _inproj_kernel,
        grid=(rows // TM_PROJ,),
        in_specs=[row_spec(D_MODEL),
                  pl.BlockSpec((3, 1, 1, D_MODEL), lambda i: (0, i // tiles_per_batch, 0, 0)),
                  _const_spec((1, D_MODEL))] + [_const_spec(w.shape) for w in weights],
        out_specs=[row_spec(n) for n in widths],
        out_shape=[jax.ShapeDtypeStruct((rows, n), dt) for n, dt in zip(widths, dtypes)],
        compiler_params=pltpu.CompilerParams(dimension_semantics=("arbitrary",),
                                             vmem_limit_bytes=VMEM_LIMIT),
        name="in_proj",
    )(x2, mod4, gain, *weights)


def _gates_kernel(g_ref, bias_ref, a_ref, nat_ref, gt_ref, ft_ref, carry_ref):
    @pl.when(pl.program_id(1) == 0)
    def _():
        carry_ref[...] = jnp.zeros_like(carry_ref)

    g = g_ref[0] + bias_ref[...]
    soft = jnp.log1p(jnp.exp(-jnp.abs(g)))
    dt = jnp.maximum(g, 0.0) + soft
    logf = jnp.minimum(g, 0.0) - soft
    lane = lax.broadcasted_iota(jnp.int32, (1, LANES), 1)
    is_dt = lane < SSM_HEADS
    is_f = (lane >= 2 * SSM_HEADS) & (lane < 2 * SSM_HEADS + ATTN_HEADS)
    src = jnp.where(is_dt, 0.0, jnp.where(lane < 2 * SSM_HEADS, dt * a_ref[...],
                                          jnp.where(is_f, logf, 0.0)))
    r_i = lax.broadcasted_iota(jnp.int32, (CHUNK, CHUNK), 0)
    c_i = lax.broadcasted_iota(jnp.int32, (CHUNK, CHUNK), 1)
    tri = (r_i >= c_i).astype(F32)
    carry = carry_ref[...]
    for ch in range(TG // CHUNK):
        rows = slice(ch * CHUNK, (ch + 1) * CHUNK)
        pref = jnp.dot(tri, src[rows], precision=lax.Precision.HIGHEST, preferred_element_type=F32)
        res = jnp.where(is_dt, dt[rows], pref + jnp.where(is_f, carry, 0.0))
        carry = res[CHUNK - 1:CHUNK, :]
        nat_ref[0, rows, :] = res
        res_t = res.T
        gt_ref[0, :, rows] = res_t[0:2 * SSM_HEADS]
        ft_ref[0, :, rows] = res_t[2 * SSM_HEADS:2 * SSM_HEADS + ATTN_HEADS]
    carry_ref[...] = carry


def _gates_call(g3, bias_row, a_row):
    bsz, seq, _ = g3.shape
    return pl.pallas_call(
        _gates_kernel,
        grid=(bsz, seq // TG),
        in_specs=[pl.BlockSpec((1, TG, GATE_COLS), lambda b, i: (b, i, 0)),
                  pl.BlockSpec((1, GATE_COLS), lambda b, i: (0, 0)),
                  pl.BlockSpec((1, GATE_COLS), lambda b, i: (0, 0))],
        out_specs=[pl.BlockSpec((1, TG, GATE_COLS), lambda b, i: (b, i, 0)),
                   pl.BlockSpec((1, 2 * SSM_HEADS, TG), lambda b, i: (b, 0, i)),
                   pl.BlockSpec((1, ATTN_HEADS, TG), lambda b, i: (b, 0, i))],
        out_shape=[jax.ShapeDtypeStruct((bsz, seq, GATE_COLS), F32),
                   jax.ShapeDtypeStruct((bsz, 2 * SSM_HEADS, seq), F32),
                   jax.ShapeDtypeStruct((bsz, ATTN_HEADS, seq), F32)],
        scratch_shapes=[pltpu.VMEM((1, GATE_COLS), F32)],
        compiler_params=pltpu.CompilerParams(dimension_semantics=("arbitrary", "arbitrary")),
        name="gate_prefix",
    )(g3, bias_row, a_row)


HALO = 8


def _ssd_kernel(xbc_ref, z_ref, nat_ref, gt_ref, cw_ref, cb_ref, dskip_ref, gain_ref,
                y_ref, ubuf, state, ybuf):
    @pl.when(pl.program_id(1) == 0)
    def _():
        ubuf[0:HALO, :] = jnp.zeros((HALO, CONV_DIM), F32)
        state[...] = jnp.zeros_like(state)

    cur = xbc_ref[0].astype(F32)
    ubuf[HALO:HALO + CHUNK, :] = cur
    conv = cw_ref[CONV_K - 1:CONV_K, :] * cur + cb_ref[...]
    for k in range(CONV_K - 1):
        off = HALO - (CONV_K - 1) + k
        conv = conv + cw_ref[k:k + 1, :] * ubuf[off:off + CHUNK, :]
    ubuf[0:HALO, :] = cur[CHUNK - HALO:CHUNK, :]
    xbc = _silu(conv)

    nat = nat_ref[0]
    gt = gt_ref[0]
    lane = lax.broadcasted_iota(jnp.int32, (1, LANES), 1)
    lo = lane < SSM_HEAD_DIM
    row_i = lax.broadcasted_iota(jnp.int32, (CHUNK, CHUNK), 0)
    col_i = lax.broadcasted_iota(jnp.int32, (CHUNK, CHUNK), 1)
    causal = row_i >= col_i
    lo_rows = row_i < SSM_HEAD_DIM
    contract_last = (((1,), (1,)), ((), ()))
    contract_first = (((0,), (0,)), ((), ()))

    for grp in range(SSM_GROUPS):
        b_g = xbc[:, D_SSM + grp * SSM_STATE:D_SSM + (grp + 1) * SSM_STATE].astype(BF16)
        c_g = xbc[:, D_SSM + D_BC + grp * SSM_STATE:D_SSM + D_BC + (grp + 1) * SSM_STATE].astype(BF16)
        cb = lax.dot_general(c_g, b_g, contract_last, preferred_element_type=F32)
        for pr in range(SSM_HEADS // SSM_GROUPS // 2):
            pair = grp * (SSM_HEADS // SSM_GROUPS // 2) + pr
            xp = xbc[:, pair * LANES:(pair + 1) * LANES]
            masks, ecol, dte, cdec = [], [], [], []
            for hd in (2 * pair, 2 * pair + 1):
                cum_col = nat[:, SSM_HEADS + hd:SSM_HEADS + hd + 1]
                dt_col = nat[:, hd:hd + 1]
                cum_row = gt[SSM_HEADS + hd:SSM_HEADS + hd + 1, :]
                dt_row = gt[hd:hd + 1, :]
                cum_last = cum_col[CHUNK - 1:CHUNK, :]
                decay = jnp.where(causal, jnp.exp(cum_col - cum_row), 0.0)
                masks.append((cb * decay * dt_row).astype(BF16))
                ecol.append(jnp.exp(cum_col))
                dte.append(dt_col * jnp.exp(cum_last - cum_col))
                cdec.append(jnp.exp(cum_last))
            yd = jnp.dot(jnp.concatenate(masks, axis=0), xp.astype(BF16),
                         preferred_element_type=F32)
            y_diag = jnp.where(lo, yd[:CHUNK], yd[CHUNK:])
            st = state[pair]
            y_off = lax.dot_general(c_g, st.astype(BF16), contract_last, preferred_element_type=F32)
            y_off = y_off * jnp.where(lo, ecol[0], ecol[1])
            xs = (xp * jnp.where(lo, dte[0], dte[1])).astype(BF16)
            upd = lax.dot_general(xs, b_g, contract_first, preferred_element_type=F32)
            state[pair] = st * jnp.where(lo_rows, cdec[0], cdec[1]) + upd
            ybuf[:, pair * LANES:(pair + 1) * LANES] = (
                y_diag + y_off + xp * dskip_ref[:, pair * LANES:(pair + 1) * LANES])

    gated = ybuf[...] * _silu(z_ref[0].astype(F32))
    width = D_SSM // SSM_GROUPS
    for grp in range(SSM_GROUPS):
        yg = gated[:, grp * width:(grp + 1) * width]
        ms = jnp.mean(yg * yg, axis=-1, keepdims=True)
        y_ref[0, :, grp * width:(grp + 1) * width] = (
            yg * lax.rsqrt(ms + EPS) * gain_ref[:, grp * width:(grp + 1) * width]).astype(y_ref.dtype)


def _ssd_call(xbc3, z3, nat, gt, conv_w, conv_b, dskip_row, gain_row):
    bsz, seq, _ = xbc3.shape
    return pl.pallas_call(
        _ssd_kernel,
        grid=(bsz, seq // CHUNK),
        in_specs=[pl.BlockSpec((1, CHUNK, CONV_DIM), lambda b, i: (b, i, 0)),
                  pl.BlockSpec((1, CHUNK, D_SSM), lambda b, i: (b, i, 0)),
                  pl.BlockSpec((1, CHUNK, GATE_COLS), lambda b, i: (b, i, 0)),
                  pl.BlockSpec((1, 2 * SSM_HEADS, CHUNK), lambda b, i: (b, 0, i)),
                  pl.BlockSpec((CONV_K, CONV_DIM), lambda b, i: (0, 0)),
                  pl.BlockSpec((1, CONV_DIM), lambda b, i: (0, 0)),
                  pl.BlockSpec((1, D_SSM), lambda b, i: (0, 0)),
                  pl.BlockSpec((1, D_SSM), lambda b, i: (0, 0))],
        out_specs=pl.BlockSpec((1, CHUNK, D_SSM), lambda b, i: (b, i, 0)),
        out_shape=jax.ShapeDtypeStruct((bsz, seq, D_SSM), BF16),
        scratch_shapes=[pltpu.VMEM((HALO + CHUNK, CONV_DIM), F32),
                        pltpu.VMEM((SSM_HEADS // 2, LANES, SSM_STATE), F32),
                        pltpu.VMEM((CHUNK, D_SSM), F32)],
        compiler_params=pltpu.CompilerParams(dimension_semantics=("arbitrary", "arbitrary")),
        name="ssd_heads",
    )(xbc3, z3, nat, gt, conv_w, conv_b, dskip_row, gain_row)


def _pair_rms(v, lo):
    sq = v * v
    s_lo = jnp.sum(jnp.where(lo, sq, 0.0), axis=-1, keepdims=True)
    s_hi = jnp.sum(jnp.where(lo, 0.0, sq), axis=-1, keepdims=True)
    return lax.rsqrt(jnp.where(lo, s_lo, s_hi) * (1.0 / ATTN_HEAD_DIM) + EPS)


def _attn_kernel(q_ref, k_ref, v_ref, f_ref, z_ref, gq_ref, gk_ref, go_ref, o_ref, kn_ref, *, seq):
    qi = pl.program_id(2)
    lane = lax.broadcasted_iota(jnp.int32, (1, LANES), 1)
    lo = lane < ATTN_HEAD_DIM

    @pl.when(qi == 0)
    def _():
        def prep(i, carry):
            rows = pl.ds(pl.multiple_of(i * K_PREP_ROWS, K_PREP_ROWS), K_PREP_ROWS)
            kf = k_ref[0, rows, :].astype(F32)
            kn_ref[rows, :] = (kf * _pair_rms(kf, lo) * gk_ref[...]).astype(BF16)
            return carry
        lax.fori_loop(0, seq // K_PREP_ROWS, prep, 0)

    qf = q_ref[0].astype(F32)
    qn = qf * _pair_rms(qf, lo) * gq_ref[...]
    qs = jnp.concatenate([jnp.where(lo, qn, 0.0), jnp.where(lo, 0.0, qn)], axis=0).astype(BF16)

    f_q0 = f_ref[0, 0, 0, pl.ds(qi, 1), :]
    f_q1 = f_ref[0, 0, 1, pl.ds(qi, 1), :]
    base0 = jnp.min(f_q0, axis=-1, keepdims=True)
    base1 = jnp.min(f_q1, axis=-1, keepdims=True)
    contract_last = (((1,), (1,)), ((), ()))

    def kv_step(j, carry, masked):
        m, l, acc = carry
        rows = pl.ds(pl.multiple_of(j * TQ, TQ), TQ)
        s = lax.dot_general(qs, kn_ref[rows, :], contract_last, preferred_element_type=F32)
        b0 = base0 - f_ref[0, 0, 0, pl.ds(j, 1), :]
        b1 = base1 - f_ref[0, 0, 1, pl.ds(j, 1), :]
        s = s + jnp.concatenate([jnp.broadcast_to(b0, (TQ, TQ)), jnp.broadcast_to(b1, (TQ, TQ))], axis=0)
        if masked:
            r_i = lax.broadcasted_iota(jnp.int32, (TQ, TQ), 0)
            c_i = lax.broadcasted_iota(jnp.int32, (TQ, TQ), 1)
            keep = c_i <= r_i
            s = jnp.where(jnp.concatenate([keep, keep], axis=0), s, NEG_BIG)
        m_new = jnp.maximum(m, jnp.max(s, axis=-1, keepdims=True))
        alpha = jnp.exp(m - m_new)
        p = jnp.exp(s - m_new)
        l = alpha * l + jnp.sum(p, axis=-1, keepdims=True)
        acc = alpha * acc + jnp.dot(p.astype(BF16), v_ref[0, rows, :], preferred_element_type=F32)
        return m_new, l, acc

    init = (jnp.full((2 * TQ, 1), NEG_BIG, F32), jnp.zeros((2 * TQ, 1), F32),
            jnp.zeros((2 * TQ, LANES), F32))
    carry = lax.fori_loop(0, qi, functools.partial(kv_step, masked=False), init)
    _, l, acc = kv_step(qi, carry, masked=True)

    on = acc * (1.0 / l)
    o = jnp.where(lo, on[:TQ], on[TQ:])
    o = o * _pair_rms(o, lo) * go_ref[...]
    o_ref[0] = (o * _silu(z_ref[0].astype(F32))).astype(o_ref.dtype)


def _attn_call(q3, k3, v3, f5, z3, gq_row, gk_row, go_row):
    bsz, seq, _ = q3.shape
    blk = lambda b, h, i: (b, i, h)
    whole = lambda b, h, i: (b, 0, h)
    return pl.pallas_call(
        functools.partial(_attn_kernel, seq=seq),
        grid=(bsz, HEAD_PAIRS, seq // TQ),
        in_specs=[pl.BlockSpec((1, TQ, LANES), blk),
                  pl.BlockSpec((1, seq, LANES), whole),
                  pl.BlockSpec((1, seq, LANES), whole),
                  pl.BlockSpec((1, 1, 2, seq // TQ, TQ), lambda b, h, i: (b, h, 0, 0, 0)),
                  pl.BlockSpec((1, TQ, LANES), blk),
                  pl.BlockSpec((1, LANES), lambda b, h, i: (0, 0)),
                  pl.BlockSpec((1, LANES), lambda b, h, i: (0, 0)),
                  pl.BlockSpec((1, LANES), lambda b, h, i: (0, h))],
        out_specs=pl.BlockSpec((1, TQ, LANES), blk),
        out_shape=jax.ShapeDtypeStruct((bsz, seq, D_ATTN), BF16),
        scratch_shapes=[pltpu.VMEM((seq, LANES), BF16)],
        compiler_params=pltpu.CompilerParams(
            dimension_semantics=("arbitrary", "arbitrary", "arbitrary"),
            vmem_limit_bytes=VMEM_LIMIT),
        name="fox_attention",
    )(q3, k3, v3, f5, z3, gq_row, gk_row, go_row)


def _outproj_kernel(x_ref, y_ref, o_ref, mod_ref, wy_ref, wo_ref, out_ref):
    mixed = jnp.dot(y_ref[...], wy_ref[...], preferred_element_type=F32)
    mixed = mixed + jnp.dot(o_ref[...], wo_ref[...], preferred_element_type=F32)
    out_ref[...] = x_ref[...] + mod_ref[2, 0] * mixed


def _outproj_call(x2, y2, o2, mod4, wy, wo, seq):
    rows = x2.shape[0]
    tiles_per_batch = seq // TM_PROJ
    row_spec = lambda n: pl.BlockSpec((TM_PROJ, n), lambda i: (i, 0))
    return pl.pallas_call(
        _outproj_kernel,
        grid=(rows // TM_PROJ,),
        in_specs=[row_spec(D_MODEL), row_spec(D_SSM), row_spec(D_ATTN),
                  pl.BlockSpec((3, 1, 1, D_MODEL), lambda i: (0, i // tiles_per_batch, 0, 0)),
                  _const_spec(wy.shape), _const_spec(wo.shape)],
        out_specs=row_spec(D_MODEL),
        out_shape=jax.ShapeDtypeStruct((rows, D_MODEL), F32),
        compiler_params=pltpu.CompilerParams(dimension_semantics=("arbitrary",),
                                             vmem_limit_bytes=VMEM_LIMIT),
        name="out_proj",
    )(x2, y2, o2, mod4, wy, wo)


def _pad_cols(w, width):
    return jnp.pad(w, ((0, 0), (0, width - w.shape[1])))


def _layer(x, c, norm_gain, w_ada, b_ada, w_in, conv_w, conv_b, dt_bias, a_log, d_skip,
           ssm_norm_gain, q_norm_gain, k_norm_gain, forget_bias, attn_norm_gain, w_out):
    bsz, seq, _ = x.shape
    assert seq % TM_PROJ == 0 and seq % TG == 0 and seq % TQ == 0 and seq % K_PREP_ROWS == 0
    x2 = x.reshape(bsz * seq, D_MODEL)

    mod4 = _ada_call(c, w_ada, b_ada).reshape(3, bsz, 1, D_MODEL)

    o_x = D_SSM
    o_dt = o_x + CONV_DIM
    o_q = o_dt + SSM_HEADS
    o_f = o_q + 4 * D_ATTN
    w_dt = w_in[:, o_dt:o_dt + SSM_HEADS]
    w_f = w_in[:, o_f:o_f + ATTN_HEADS]
    weights = [w_in[:, :D_SSM], w_in[:, o_x:o_x + CONV_DIM],
               _pad_cols(jnp.concatenate([w_dt, w_dt, w_f], axis=1), GATE_COLS),
               w_in[:, o_q:o_q + D_ATTN], w_in[:, o_q + D_ATTN:o_q + 2 * D_ATTN],
               w_in[:, o_q + 2 * D_ATTN:o_q + 3 * D_ATTN], w_in[:, o_q + 3 * D_ATTN:o_q + 4 * D_ATTN]]
    weights = [w.astype(BF16) for w in weights]
    z_ssd, xbc, graw, q, k, v, z_attn = _inproj_call(
        x2, mod4, norm_gain.reshape(1, D_MODEL), weights, seq)

    bias_row = _pad_cols(jnp.concatenate([dt_bias, dt_bias, forget_bias]).reshape(1, -1), GATE_COLS)
    a_neg = -jnp.exp(a_log.astype(F32))
    a_row = _pad_cols(jnp.concatenate([jnp.zeros_like(a_neg), a_neg]).reshape(1, -1), GATE_COLS)
    nat, gt, ft = _gates_call(graw.reshape(bsz, seq, GATE_COLS), bias_row.astype(F32), a_row)

    to3 = lambda t: t.reshape(bsz, seq, t.shape[-1])
    y = _ssd_call(to3(xbc), to3(z_ssd), nat, gt, conv_w, conv_b.reshape(1, CONV_DIM),
                  jnp.repeat(d_skip, SSM_HEAD_DIM).reshape(1, D_SSM),
                  ssm_norm_gain.reshape(1, D_SSM))

    f5 = ft.reshape(bsz, HEAD_PAIRS, 2, seq // TQ, TQ)
    gq_row = jnp.tile(q_norm_gain, 2).reshape(1, LANES) * (1.0 / math.sqrt(ATTN_HEAD_DIM))
    gk_row = jnp.tile(k_norm_gain, 2).reshape(1, LANES)
    o = _attn_call(to3(q), to3(k), to3(v), f5, to3(z_attn), gq_row, gk_row,
                   attn_norm_gain.reshape(1, D_ATTN))

    out = _outproj_call(x2, y.reshape(bsz * seq, D_SSM), o.reshape(bsz * seq, D_ATTN), mod4,
                        w_out[:D_SSM].astype(BF16), w_out[D_SSM:].astype(BF16), seq)
    return out.reshape(bsz, seq, D_MODEL)


def kernel(x, c, norm_gain, w_ada, b_ada, w_in, conv_w, conv_b, dt_bias, a_log, d_skip,
           ssm_norm_gain, q_norm_gain, k_norm_gain, forget_bias, attn_norm_gain, w_out):
    for layer in range(norm_gain.shape[0]):
        x = _layer(x, c, norm_gain[layer], w_ada[layer], b_ada[layer], w_in[layer],
                   conv_w[layer], conv_b[layer], dt_bias[layer], a_log[layer], d_skip[layer],
                   ssm_norm_gain[layer], q_norm_gain[layer], k_norm_gain[layer],
                   forget_bias[layer], attn_norm_gain[layer], w_out[layer])
    return x
```

```python
import functools
import math

import jax
import jax.numpy as jnp
from jax import lax
from jax.experimental import pallas as pl
from jax.experimental.pallas import tpu as pltpu

F32 = jnp.float32
BF16 = jnp.bfloat16

D_MODEL = 1024
D_SSM = 1024
D_ATTN = 1024
SSM_HEAD_DIM = 64
SSM_HEADS = 16
SSM_GROUPS = 2
SSM_STATE = 128
CONV_K = 4
CHUNK = 128
D_BC = SSM_GROUPS * SSM_STATE
CONV_DIM = D_SSM + 2 * D_BC
ATTN_HEAD_DIM = 64
ATTN_HEADS = 16
EPS = 1e-6
LANES = 128
HEAD_PAIRS = ATTN_HEADS // 2
GATE_COLS = LANES
F_LANE0 = 2 * SSM_HEADS
F_SPLIT = 3
LOG2E = math.log2(math.e)

TM_PROJ = 512
TG = 512
TQ = 1024
TK = TG
KT_PER_QT = TQ // TK
QT = 256
V_ROWS = LANES + 16
K_PREP_ROWS = 1024
NEG_BIG = -1e30
VMEM_LIMIT = 56 * 1024 * 1024


def _silu(v):
    return v * jax.nn.sigmoid(v)


def _const_spec(shape):
    nd = len(shape)
    return pl.BlockSpec(shape, lambda *_: (0,) * nd, pipeline_mode=pl.Buffered(1))


def _ada_kernel(c_ref, w_ref, b_ref, o_ref):
    c = c_ref[...]
    o_ref[0] = jnp.dot(_silu(c), w_ref[...], precision=lax.Precision.HIGHEST,
                       preferred_element_type=F32) + b_ref[0]


def _ada_call(c, w_ada, b_ada):
    bsz = c.shape[0]
    return pl.pallas_call(
        _ada_kernel,
        grid=(3,),
        in_specs=[pl.BlockSpec((bsz, D_MODEL), lambda j: (0, 0)),
                  pl.BlockSpec((D_MODEL, D_MODEL), lambda j: (0, j)),
                  pl.BlockSpec((1, 1, D_MODEL), lambda j: (j, 0, 0))],
        out_specs=pl.BlockSpec((1, bsz, D_MODEL), lambda j: (j, 0, 0)),
        out_shape=jax.ShapeDtypeStruct((3, bsz, D_MODEL), F32),
        name="adaln_mod",
    )(c, w_ada, b_ada.reshape(3, 1, D_MODEL))


def _inproj_kernel(x_ref, mod_ref, gain_ref, wz, wxbc, wg, wq, wk, wvt, wza,
                   oz, oxbc, og, oq, ok, ovt, oza):
    x = x_ref[...]
    ms = jnp.mean(x * x, axis=-1, keepdims=True)
    shift = mod_ref[0, 0]
    scale = mod_ref[1, 0]
    h = (x * lax.rsqrt(ms + EPS) * gain_ref[...]) * (1.0 + scale) + shift
    hb = h.astype(BF16)
    for w_ref, o_ref in ((wz, oz), (wxbc, oxbc), (wg, og), (wq, oq), (wk, ok), (wza, oza)):
        o_ref[...] = jnp.dot(hb, w_ref[...], preferred_element_type=F32).astype(o_ref.dtype)
    ovt[0] = lax.dot_general(wvt[...], hb, (((1,), (1,)), ((), ())),
                             preferred_element_type=F32).astype(ovt.dtype)


def _inproj_call(x2, mod4, gain, weights, bsz, seq):
    rows = x2.shape[0]
    tiles_per_batch = seq // TM_PROJ
    w_z, w_xbc, w_g, w_q, w_k, w_vt, w_za = weights
    row_spec = lambda n: pl.BlockSpec((TM_PROJ, n), lambda i: (i, 0))
    row_out = lambda n, dt: jax.ShapeDtypeStruct((rows, n), dt)
    return pl.pallas_call(
        _inproj_kernel,
        grid=(rows // TM_PROJ,),
        in_specs=[row_spec(D_MODEL),
                  pl.BlockSpec((3, 1, 1, D_MODEL), lambda i: (0, i // tiles_per_batch, 0, 0)),
                  _const_spec((1, D_MODEL))] + [_const_spec(w.shape) for w in weights],
        out_specs=[row_spec(D_SSM), row_spec(CONV_DIM), row_spec(GATE_COLS), row_spec(D_ATTN),
                   row_spec(D_ATTN),
                   pl.BlockSpec((1, D_ATTN, TM_PROJ),
                                lambda i: (i // tiles_per_batch, 0, i % tiles_per_batch)),
                   row_spec(D_ATTN)],
        out_shape=[row_out(D_SSM, BF16), row_out(CONV_DIM, BF16), row_out(GATE_COLS, F32),
                   row_out(D_ATTN, BF16), row_out(D_ATTN, BF16),
                   jax.ShapeDtypeStruct((bsz, D_ATTN, seq), BF16), row_out(D_ATTN, BF16)],
        compiler_params=pltpu.CompilerParams(dimension_semantics=("arbitrary",),
                                             vmem_limit_bytes=VMEM_LIMIT),
        name="in_proj",
    )(x2, mod4, gain, *weights)


def _gates_kernel(g_ref, plan_ref, nat_ref, gt_ref, faug_ref, ftab_ref, carry_ref):
    @pl.when(pl.program_id(1) == 0)
    def _():
        carry_ref[...] = jnp.zeros_like(carry_ref)

    g = g_ref[0] + plan_ref[0:1, :]
    soft = jnp.log1p(jnp.exp(-jnp.abs(g)))
    dt = jnp.maximum(g, 0.0) + soft
    logf = jnp.minimum(g, 0.0) - soft
    lane = lax.broadcasted_iota(jnp.int32, (1, LANES), 1)
    is_dt = lane < SSM_HEADS
    is_f = (lane >= F_LANE0) & (lane < F_LANE0 + F_SPLIT * ATTN_HEADS)
    src = jnp.where(is_dt, 0.0, jnp.where(lane < F_LANE0, dt * plan_ref[1:2, :],
                                          jnp.where(is_f, logf, 0.0)))
    r_i = lax.broadcasted_iota(jnp.int32, (CHUNK, CHUNK), 0)
    c_i = lax.broadcasted_iota(jnp.int32, (CHUNK, CHUNK), 1)
    tri = (r_i >= c_i).astype(F32)
    carry = carry_ref[...]
    chunks = []
    for ch in range(TG // CHUNK):
        rows = slice(ch * CHUNK, (ch + 1) * CHUNK)
        pref = jnp.dot(tri, src[rows], precision=lax.Precision.HIGHEST, preferred_element_type=F32)
        res = jnp.where(is_dt, dt[rows], pref + jnp.where(is_f, carry, 0.0))
        carry = res[CHUNK - 1:CHUNK, :]
        nat_ref[0, rows, :] = res
        gt_ref[0, :, rows] = res.T[0:2 * SSM_HEADS]
        chunks.append(res)
    carry_ref[...] = carry

    fcum = jnp.concatenate(chunks, axis=0)
    first = fcum[0:1, :]
    ftab_ref[0, 0] = first
    d = (first - fcum) * LOG2E
    hi = d.astype(BF16).astype(F32)
    r1 = d - hi
    mid = r1.astype(BF16).astype(F32)
    lo = (r1 - mid).astype(BF16).astype(F32)
    faug_ref[0] = (plan_ref[2:3, :] * hi + plan_ref[3:4, :] * mid + plan_ref[4:5, :] * lo).astype(BF16)


def _gates_call(g3, plan):
    bsz, seq, _ = g3.shape
    return pl.pallas_call(
        _gates_kernel,
        grid=(bsz, seq // TG),
        in_specs=[pl.BlockSpec((1, TG, GATE_COLS), lambda b, i: (b, i, 0)),
                  pl.BlockSpec((8, GATE_COLS), lambda b, i: (0, 0))],
        out_specs=[pl.BlockSpec((1, TG, GATE_COLS), lambda b, i: (b, i, 0)),
                   pl.BlockSpec((1, 2 * SSM_HEADS, TG), lambda b, i: (b, 0, i)),
                   pl.BlockSpec((1, TG, GATE_COLS), lambda b, i: (b, i, 0)),
                   pl.BlockSpec((1, 1, 1, GATE_COLS), lambda b, i: (b, i, 0, 0))],
        out_shape=[jax.ShapeDtypeStruct((bsz, seq, GATE_COLS), F32),
                   jax.ShapeDtypeStruct((bsz, 2 * SSM_HEADS, seq), F32),
                   jax.ShapeDtypeStruct((bsz, seq, GATE_COLS), BF16),
                   jax.ShapeDtypeStruct((bsz, seq // TG, 1, GATE_COLS), F32)],
        scratch_shapes=[pltpu.VMEM((1, GATE_COLS), F32)],
        compiler_params=pltpu.CompilerParams(dimension_semantics=("arbitrary", "arbitrary")),
        name="gate_prefix",
    )(g3, plan)


HALO = 8


def _ssd_kernel(xbc_ref, z_ref, nat_ref, gt_ref, cw_ref, cb_ref, dskip_ref, gain_ref,
                y_ref, ubuf, state, ybuf):
    @pl.when(pl.program_id(1) == 0)
    def _():
        ubuf[0:HALO, :] = jnp.zeros((HALO, CONV_DIM), F32)
        state[...] = jnp.zeros_like(state)

    cur = xbc_ref[0].astype(F32)
    ubuf[HALO:HALO + CHUNK, :] = cur
    conv = cw_ref[CONV_K - 1:CONV_K, :] * cur + cb_ref[...]
    for k in range(CONV_K - 1):
        off = HALO - (CONV_K - 1) + k
        conv = conv + cw_ref[k:k + 1, :] * ubuf[off:off + CHUNK, :]
    ubuf[0:HALO, :] = cur[CHUNK - HALO:CHUNK, :]
    xbc = _silu(conv)

    nat = nat_ref[0]
    gt = gt_ref[0]
    lane = lax.broadcasted_iota(jnp.int32, (1, LANES), 1)
    lo = lane < SSM_HEAD_DIM
    row_i = lax.broadcasted_iota(jnp.int32, (CHUNK, CHUNK), 0)
    col_i = lax.broadcasted_iota(jnp.int32, (CHUNK, CHUNK), 1)
    causal = row_i >= col_i
    lo_rows = row_i < SSM_HEAD_DIM
    contract_last = (((1,), (1,)), ((), ()))
    contract_first = (((0,), (0,)), ((), ()))

    for grp in range(SSM_GROUPS):
        b_g = xbc[:, D_SSM + grp * SSM_STATE:D_SSM + (grp + 1) * SSM_STATE].astype(BF16)
        c_g = xbc[:, D_SSM + D_BC + grp * SSM_STATE:D_SSM + D_BC + (grp + 1) * SSM_STATE].astype(BF16)
        cb = lax.dot_general(c_g, b_g, contract_last, preferred_element_type=F32)
        for pr in range(SSM_HEADS // SSM_GROUPS // 2):
            pair = grp * (SSM_HEADS // SSM_GROUPS // 2) + pr
            xp = xbc[:, pair * LANES:(pair + 1) * LANES]
            masks, ecol, dte, cdec = [], [], [], []
            for hd in (2 * pair, 2 * pair + 1):
                cum_col = nat[:, SSM_HEADS + hd:SSM_HEADS + hd + 1]
                dt_col = nat[:, hd:hd + 1]
                cum_row = gt[SSM_HEADS + hd:SSM_HEADS + hd + 1, :]
                dt_row = gt[hd:hd + 1, :]
                cum_last = cum_col[CHUNK - 1:CHUNK, :]
                decay = jnp.where(causal, jnp.exp(cum_col - cum_row), 0.0)
                masks.append((cb * decay * dt_row).astype(BF16))
                ecol.append(jnp.exp(cum_col))
                dte.append(dt_col * jnp.exp(cum_last - cum_col))
                cdec.append(jnp.exp(cum_last))
            yd = jnp.dot(jnp.concatenate(masks, axis=0), xp.astype(BF16),
                         preferred_element_type=F32)
            y_diag = jnp.where(lo, yd[:CHUNK], yd[CHUNK:])
            st = state[pair]
            y_off = lax.dot_general(c_g, st.astype(BF16), contract_last, preferred_element_type=F32)
            y_off = y_off * jnp.where(lo, ecol[0], ecol[1])
            xs = (xp * jnp.where(lo, dte[0], dte[1])).astype(BF16)
            upd = lax.dot_general(xs, b_g, contract_first, preferred_element_type=F32)
            state[pair] = st * jnp.where(lo_rows, cdec[0], cdec[1]) + upd
            ybuf[:, pair * LANES:(pair + 1) * LANES] = (
                y_diag + y_off + xp * dskip_ref[:, pair * LANES:(pair + 1) * LANES])

    gated = ybuf[...] * _silu(z_ref[0].astype(F32))
    width = D_SSM // SSM_GROUPS
    for grp in range(SSM_GROUPS):
        yg = gated[:, grp * width:(grp + 1) * width]
        ms = jnp.mean(yg * yg, axis=-1, keepdims=True)
        y_ref[0, :, grp * width:(grp + 1) * width] = (
            yg * lax.rsqrt(ms + EPS) * gain_ref[:, grp * width:(grp + 1) * width]).astype(y_ref.dtype)


def _ssd_call(xbc3, z3, nat, gt, conv_w, conv_b, dskip_row, gain_row):
    bsz, seq, _ = xbc3.shape
    return pl.pallas_call(
        _ssd_kernel,
        grid=(bsz, seq // CHUNK),
        in_specs=[pl.BlockSpec((1, CHUNK, CONV_DIM), lambda b, i: (b, i, 0)),
                  pl.BlockSpec((1, CHUNK, D_SSM), lambda b, i: (b, i, 0)),
                  pl.BlockSpec((1, CHUNK, GATE_COLS), lambda b, i: (b, i, 0)),
                  pl.BlockSpec((1, 2 * SSM_HEADS, CHUNK), lambda b, i: (b, 0, i)),
                  pl.BlockSpec((CONV_K, CONV_DIM), lambda b, i: (0, 0)),
                  pl.BlockSpec((1, CONV_DIM), lambda b, i: (0, 0)),
                  pl.BlockSpec((1, D_SSM), lambda b, i: (0, 0)),
                  pl.BlockSpec((1, D_SSM), lambda b, i: (0, 0))],
        out_specs=pl.BlockSpec((1, CHUNK, D_SSM), lambda b, i: (b, i, 0)),
        out_shape=jax.ShapeDtypeStruct((bsz, seq, D_SSM), BF16),
        scratch_shapes=[pltpu.VMEM((HALO + CHUNK, CONV_DIM), F32),
                        pltpu.VMEM((SSM_HEADS // 2, LANES, SSM_STATE), F32),
                        pltpu.VMEM((CHUNK, D_SSM), F32)],
        compiler_params=pltpu.CompilerParams(dimension_semantics=("arbitrary", "arbitrary")),
        name="ssd_heads",
    )(xbc3, z3, nat, gt, conv_w, conv_b, dskip_row, gain_row)


def _pair_rms(v, lo):
    sq = v * v
    s_lo = jnp.sum(jnp.where(lo, sq, 0.0), axis=-1, keepdims=True)
    s_hi = jnp.sum(jnp.where(lo, 0.0, sq), axis=-1, keepdims=True)
    return lax.rsqrt(jnp.where(lo, s_lo, s_hi) * (1.0 / ATTN_HEAD_DIM) + EPS)


def _attn_kernel(qi_tab, j_tab, ftab_ref, q_ref, k_ref, vt_ref, faug_ref, z_ref, gq_ref, gk_ref,
                 go_ref, o_ref, kaug_ref, vaug_ref, qs_ref, s0_ref, s1_ref, cmax0_ref, cmax1_ref,
                 acc_ref, *, seq):
    b = pl.program_id(0)
    hp = pl.program_id(1)
    nblk = seq // TK
    n_steps = qi_tab.shape[0]
    n_qt = 2 * TQ // QT
    s_refs, cmax_refs = (s0_ref, s1_ref), (cmax0_ref, cmax1_ref)
    lane = lax.broadcasted_iota(jnp.int32, (1, LANES), 1)
    lo = lane < ATTN_HEAD_DIM
    contract_last = (((1,), (1,)), ((), ()))

    def prep(i, carry):
        rows = pl.ds(pl.multiple_of(i * K_PREP_ROWS, K_PREP_ROWS), K_PREP_ROWS)
        kf = k_ref[0, rows, :].astype(F32)
        kaug_ref[rows, 0:LANES] = (kf * _pair_rms(kf, lo) * gk_ref[...]).astype(BF16)
        kaug_ref[rows, LANES:2 * LANES] = faug_ref[0, rows, :]
        vaug_ref[0:LANES, rows] = vt_ref[0, :, rows]
        return carry
    lax.fori_loop(0, seq // K_PREP_ROWS, prep, 0)
    vaug_ref[LANES:V_ROWS, :] = jnp.ones((V_ROWS - LANES, seq), BF16)
    acc_ref[...] = jnp.zeros_like(acc_ref)

    def build_qs(qi):
        rows = pl.ds(pl.multiple_of(qi * TQ, TQ), TQ)
        qf = q_ref[0, rows, :].astype(F32)
        qn = qf * _pair_rms(qf, lo) * gq_ref[...]
        slot = qi % 2
        for half in range(2):
            f0 = F_LANE0 + F_SPLIT * (2 * hp + half)
            ones = jnp.where((lane >= f0) & (lane < f0 + F_SPLIT), 1.0, 0.0)
            keep = lo if half == 0 else jnp.logical_not(lo)
            qs_ref[slot, half * TQ:(half + 1) * TQ, 0:LANES] = jnp.where(keep, qn, 0.0).astype(BF16)
            qs_ref[slot, half * TQ:(half + 1) * TQ, LANES:2 * LANES] = (
                jnp.broadcast_to(ones, (TQ, LANES)).astype(BF16))

    tab0 = (b * ATTN_HEADS + 2 * hp) * nblk

    def tile_offset(qi, j, qt):
        tab = tab0 + (qt * QT // TQ) * nblk
        base = jnp.full((1, QT), ftab_ref[tab + qi * KT_PER_QT])
        return (base - jnp.full((1, QT), ftab_ref[tab + j])) * LOG2E

    def stage_a(n, slot, qt, diag):
        qi, j = qi_tab[n], j_tab[n]
        start = pl.multiple_of(j * TK, TK)
        cols = slice(qt * QT, (qt + 1) * QT)
        st = lax.dot_general(kaug_ref[pl.ds(start, TK), :], qs_ref[qi % 2, cols, :], contract_last,
                             preferred_element_type=F32)
        q_first = (qt * QT) % TQ
        if diag is not None and q_first < diag * TK + TK - 1:
            k_i = lax.broadcasted_iota(jnp.int32, (TK, QT), 0) + diag * TK
            q_i = lax.broadcasted_iota(jnp.int32, (TK, QT), 1) + q_first
            st = jnp.where(k_i <= q_i, st, NEG_BIG)
        s_refs[slot][:, cols] = st
        cmax_refs[slot][:, cols] = jnp.max(st, axis=0, keepdims=True)

    def stage_bc(n, slot, qt, m):
        qi, j = qi_tab[n], j_tab[n]
        start = pl.multiple_of(j * TK, TK)
        cols = slice(qt * QT, (qt + 1) * QT)
        m = jnp.where(j == 0, NEG_BIG, m)
        c = tile_offset(qi, j, qt)
        m_new = jnp.maximum(m, cmax_refs[slot][:, cols] + c)
        alpha = jnp.exp2(m - m_new)
        p = jnp.exp2(s_refs[slot][:, cols] - (m_new - c)).astype(BF16)
        pv = jnp.dot(vaug_ref[:, pl.ds(start, TK)], p, preferred_element_type=F32)
        acc_ref[:, cols] = alpha * acc_ref[:, cols] + pv
        return m_new

    def finalize(qi):
        rows = pl.ds(pl.multiple_of(qi * TQ, TQ), TQ)
        accn = acc_ref[0:LANES, :] * (1.0 / acc_ref[LANES:LANES + 1, :])
        o_t = jnp.concatenate([accn[0:ATTN_HEAD_DIM, 0:TQ], accn[ATTN_HEAD_DIM:, TQ:]], axis=0)
        o = o_t.T
        o = o * _pair_rms(o, lo) * go_ref[...]
        o_ref[0, rows, :] = (o * _silu(z_ref[0, rows, :].astype(F32))).astype(o_ref.dtype)

    def iteration(t, parity, carry):
        static = isinstance(t, int)
        do_a = (t < n_steps) if static else True
        do_bc = (t >= 1) if static else True
        if do_a:
            @pl.when(j_tab[t] == 0)
            def _():
                build_qs(qi_tab[t])

        def stages(carry, diag):
            carry = list(carry)
            for qt in range(n_qt):
                if do_a:
                    stage_a(t, parity, qt, diag)
                if do_bc:
                    carry[qt] = stage_bc(t - 1, 1 - parity, qt, carry[qt])
            return tuple(carry)

        if do_a:
            variants = [functools.partial(stages, diag=None)]
            variants += [functools.partial(stages, diag=d) for d in range(KT_PER_QT)]
            kind = jnp.maximum(j_tab[t] - qi_tab[t] * KT_PER_QT + 1, 0)
            carry = lax.switch(kind, variants, carry)
        else:
            carry = stages(carry, None)
        if do_bc:
            @pl.when(j_tab[t - 1] == qi_tab[t - 1] * KT_PER_QT + KT_PER_QT - 1)
            def _():
                finalize(qi_tab[t - 1])
        return carry

    carry = tuple(jnp.full((1, QT), NEG_BIG, F32) for _ in range(n_qt))
    n_pairs = max(n_steps - 1, 0) // 2
    carry = iteration(0, 0, carry)

    def pair_body(i, carry):
        t = 1 + 2 * i
        carry = iteration(t, 1, carry)
        return iteration(t + 1, 0, carry)
    carry = lax.fori_loop(0, n_pairs, pair_body, carry)
    for t in range(1 + 2 * n_pairs, n_steps + 1):
        carry = iteration(t, t % 2, carry)


def _attn_call(ftab, q3, k3, vt3, faug, z3, gq_row, gk_row, go_row):
    bsz, seq, _ = q3.shape
    steps = [(qi, j) for qi in range(seq // TQ) for j in range((qi + 1) * KT_PER_QT)]
    qi_tab = jnp.asarray([s[0] for s in steps], jnp.int32)
    j_tab = jnp.asarray([s[1] for s in steps], jnp.int32)
    smem = pl.BlockSpec(memory_space=pltpu.SMEM)
    rows_spec = lambda imap: pl.BlockSpec((1, seq, LANES), imap)
    return pl.pallas_call(
        functools.partial(_attn_kernel, seq=seq),
        grid=(bsz, HEAD_PAIRS),
        in_specs=[smem, smem, smem,
                  rows_spec(lambda b, h: (b, 0, h)),
                  rows_spec(lambda b, h: (b, 0, h)),
                  pl.BlockSpec((1, LANES, seq), lambda b, h: (b, h, 0)),
                  rows_spec(lambda b, h: (b, 0, 0)),
                  rows_spec(lambda b, h: (b, 0, h)),
                  pl.BlockSpec((1, LANES), lambda b, h: (0, 0)),
                  pl.BlockSpec((1, LANES), lambda b, h: (0, 0)),
                  pl.BlockSpec((1, LANES), lambda b, h: (0, h))],
        out_specs=rows_spec(lambda b, h: (b, 0, h)),
        out_shape=jax.ShapeDtypeStruct((bsz, seq, D_ATTN), BF16),
        scratch_shapes=[pltpu.VMEM((seq, 2 * LANES), BF16),
                        pltpu.VMEM((V_ROWS, seq), BF16),
                        pltpu.VMEM((2, 2 * TQ, 2 * LANES), BF16),
                        pltpu.VMEM((TK, 2 * TQ), F32), pltpu.VMEM((TK, 2 * TQ), F32),
                        pltpu.VMEM((1, 2 * TQ), F32), pltpu.VMEM((1, 2 * TQ), F32),
                        pltpu.VMEM((V_ROWS, 2 * TQ), F32)],
        compiler_params=pltpu.CompilerParams(
            dimension_semantics=("arbitrary", "arbitrary"),
            vmem_limit_bytes=VMEM_LIMIT),
        name="fox_attention",
    )(qi_tab, j_tab, ftab, q3, k3, vt3, faug, z3, gq_row, gk_row, go_row)


def _outproj_kernel(x_ref, y_ref, o_ref, mod_ref, wy_ref, wo_ref, out_ref):
    mixed = jnp.dot(y_ref[...], wy_ref[...], preferred_element_type=F32)
    mixed = mixed + jnp.dot(o_ref[...], wo_ref[...], preferred_element_type=F32)
    out_ref[...] = x_ref[...] + mod_ref[2, 0] * mixed


def _outproj_call(x2, y2, o2, mod4, wy, wo, seq):
    rows = x2.shape[0]
    tiles_per_batch = seq // TM_PROJ
    row_spec = lambda n: pl.BlockSpec((TM_PROJ, n), lambda i: (i, 0))
    return pl.pallas_call(
        _outproj_kernel,
        grid=(rows // TM_PROJ,),
        in_specs=[row_spec(D_MODEL), row_spec(D_SSM), row_spec(D_ATTN),
                  pl.BlockSpec((3, 1, 1, D_MODEL), lambda i: (0, i // tiles_per_batch, 0, 0)),
                  _const_spec(wy.shape), _const_spec(wo.shape)],
        out_specs=row_spec(D_MODEL),
        out_shape=jax.ShapeDtypeStruct((rows, D_MODEL), F32),
        compiler_params=pltpu.CompilerParams(dimension_semantics=("arbitrary",),
                                             vmem_limit_bytes=VMEM_LIMIT),
        name="out_proj",
    )(x2, y2, o2, mod4, wy, wo)


def _pad_cols(w, width):
    return jnp.pad(w, ((0, 0), (0, width - w.shape[1])))


def _layer(x, c, norm_gain, w_ada, b_ada, w_in, conv_w, conv_b, dt_bias, a_log, d_skip,
           ssm_norm_gain, q_norm_gain, k_norm_gain, forget_bias, attn_norm_gain, w_out):
    bsz, seq, _ = x.shape
    assert seq % TM_PROJ == 0 and seq % TG == 0 and seq % TQ == 0 and seq % K_PREP_ROWS == 0
    assert TQ % TK == 0 and TQ % QT == 0
    x2 = x.reshape(bsz * seq, D_MODEL)

    mod4 = _ada_call(c, w_ada, b_ada).reshape(3, bsz, 1, D_MODEL)

    o_x = D_SSM
    o_dt = o_x + CONV_DIM
    o_q = o_dt + SSM_HEADS
    o_f = o_q + 4 * D_ATTN
    w_dt = w_in[:, o_dt:o_dt + SSM_HEADS]
    w_f3 = jnp.repeat(w_in[:, o_f:o_f + ATTN_HEADS], F_SPLIT, axis=1)
    weights = [w_in[:, :D_SSM], w_in[:, o_x:o_x + CONV_DIM],
               _pad_cols(jnp.concatenate([w_dt, w_dt, w_f3], axis=1), GATE_COLS),
               w_in[:, o_q:o_q + D_ATTN], w_in[:, o_q + D_ATTN:o_q + 2 * D_ATTN],
               w_in[:, o_q + 2 * D_ATTN:o_q + 3 * D_ATTN].T, w_in[:, o_q + 3 * D_ATTN:o_q + 4 * D_ATTN]]
    weights = [w.astype(BF16) for w in weights]
    z_ssd, xbc, graw, q, k, vt, z_attn = _inproj_call(
        x2, mod4, norm_gain.reshape(1, D_MODEL), weights, bsz, seq)

    a_neg = -jnp.exp(a_log.astype(F32))
    f_lane = jnp.arange(F_SPLIT * ATTN_HEADS) % F_SPLIT
    plan_rows = [jnp.concatenate([dt_bias, dt_bias, jnp.repeat(forget_bias, F_SPLIT)]),
                 jnp.concatenate([jnp.zeros_like(a_neg), a_neg])]
    plan_rows += [jnp.concatenate([jnp.zeros((F_LANE0,), F32), (f_lane == t).astype(F32)])
                  for t in range(F_SPLIT)]
    plan = jnp.stack([jnp.pad(r.astype(F32), (0, GATE_COLS - r.shape[0])) for r in plan_rows])
    plan = jnp.pad(plan, ((0, 8 - plan.shape[0]), (0, 0)))
    nat, gt, faug, ftab = _gates_call(graw.reshape(bsz, seq, GATE_COLS), plan)

    to3 = lambda t: t.reshape(bsz, seq, t.shape[-1])
    y = _ssd_call(to3(xbc), to3(z_ssd), nat, gt, conv_w, conv_b.reshape(1, CONV_DIM),
                  jnp.repeat(d_skip, SSM_HEAD_DIM).reshape(1, D_SSM),
                  ssm_norm_gain.reshape(1, D_SSM))

    ftab_flat = jnp.transpose(ftab[:, :, 0, F_LANE0:F_LANE0 + F_SPLIT * ATTN_HEADS:F_SPLIT],
                              (0, 2, 1)).reshape(-1)
    gq_row = jnp.tile(q_norm_gain, 2).reshape(1, LANES) * (LOG2E / math.sqrt(ATTN_HEAD_DIM))
    gk_row = jnp.tile(k_norm_gain, 2).reshape(1, LANES)
    o = _attn_call(ftab_flat, to3(q), to3(k), vt, faug, to3(z_attn), gq_row, gk_row,
                   attn_norm_gain.reshape(1, D_ATTN))

    out = _outproj_call(x2, y.reshape(bsz * seq, D_SSM), o.reshape(bsz * seq, D_ATTN), mod4,
                        w_out[:D_SSM].astype(BF16), w_out[D_SSM:].astype(BF16), seq)
    return out.reshape(bsz, seq, D_MODEL)


def kernel(x, c, norm_gain, w_ada, b_ada, w_in, conv_w, conv_b, dt_bias, a_log, d_skip,
           ssm_norm_gain, q_norm_gain, k_norm_gain, forget_bias, attn_norm_gain, w_out):
    for layer in range(norm_gain.shape[0]):
        x = _layer(x, c, norm_gain[layer], w_ada[layer], b_ada[layer], w_in[layer],
                   conv_w[layer], conv_b[layer], dt_bias[layer], a_log[layer], d_skip[layer],
                   ssm_norm_gain[layer], q_norm_gain[layer], k_norm_gain[layer],
                   forget_bias[layer], attn_norm_gain[layer], w_out[layer])
    return x
```

```python
import functools
import math

import jax
import jax.numpy as jnp
from jax import lax
from jax.experimental import pallas as pl
from jax.experimental.pallas import tpu as pltpu

F32 = jnp.float32
BF16 = jnp.bfloat16

D_MODEL = 1024
D_SSM = 1024
D_ATTN = 1024
SSM_HEAD_DIM = 64
SSM_HEADS = 16
SSM_GROUPS = 2
SSM_STATE = 128
CONV_K = 4
CHUNK = 128
D_BC = SSM_GROUPS * SSM_STATE
CONV_DIM = D_SSM + 2 * D_BC
ATTN_HEAD_DIM = 64
ATTN_HEADS = 16
EPS = 1e-6
LANES = 128
HEAD_PAIRS = ATTN_HEADS // 2
GATE_COLS = LANES
F_LANE0 = 2 * SSM_HEADS
F_SPLIT = 3
LOG2E = math.log2(math.e)

TM_PROJ = 512
TG = 512
TQ = 1024
TK = TG
KT_PER_QT = TQ // TK
QT = 256
V_ROWS = LANES + 16
K_PREP_ROWS = 1024
NEG_BIG = -1e30
VMEM_LIMIT = 56 * 1024 * 1024


def _silu(v):
    return v * jax.nn.sigmoid(v)


def _const_spec(shape):
    nd = len(shape)
    return pl.BlockSpec(shape, lambda *_: (0,) * nd, pipeline_mode=pl.Buffered(1))


def _ada_kernel(c_ref, w_ref, b_ref, o_ref):
    c = c_ref[...]
    o_ref[0] = jnp.dot(_silu(c), w_ref[...], precision=lax.Precision.HIGHEST,
                       preferred_element_type=F32) + b_ref[0]


def _ada_call(c, w_ada, b_ada):
    bsz = c.shape[0]
    return pl.pallas_call(
        _ada_kernel,
        grid=(3,),
        in_specs=[pl.BlockSpec((bsz, D_MODEL), lambda j: (0, 0)),
                  pl.BlockSpec((D_MODEL, D_MODEL), lambda j: (0, j)),
                  pl.BlockSpec((1, 1, D_MODEL), lambda j: (j, 0, 0))],
        out_specs=pl.BlockSpec((1, bsz, D_MODEL), lambda j: (j, 0, 0)),
        out_shape=jax.ShapeDtypeStruct((3, bsz, D_MODEL), F32),
        name="adaln_mod",
    )(c, w_ada, b_ada.reshape(3, 1, D_MODEL))


def _pair_rms(v, lo):
    sq = v * v
    s_lo = jnp.sum(jnp.where(lo, sq, 0.0), axis=-1, keepdims=True)
    s_hi = jnp.sum(jnp.where(lo, 0.0, sq), axis=-1, keepdims=True)
    return lax.rsqrt(jnp.where(lo, s_lo, s_hi) * (1.0 / ATTN_HEAD_DIM) + EPS)


def _inproj_kernel(x_ref, mod_ref, gain_ref, gqk_ref, wz, wxbc, wg, wq, wk, wvt, wza,
                   oz, oxbc, og, oq, ok, ovt, oza):
    x = x_ref[...]
    ms = jnp.mean(x * x, axis=-1, keepdims=True)
    shift = mod_ref[0, 0]
    scale = mod_ref[1, 0]
    h = (x * lax.rsqrt(ms + EPS) * gain_ref[...]) * (1.0 + scale) + shift
    hb = h.astype(BF16)
    for w_ref, o_ref in ((wz, oz), (wxbc, oxbc), (wg, og), (wza, oza)):
        o_ref[...] = jnp.dot(hb, w_ref[...], preferred_element_type=F32).astype(o_ref.dtype)
    lo = lax.broadcasted_iota(jnp.int32, (1, LANES), 1) < ATTN_HEAD_DIM
    for row, (w_ref, o_ref) in enumerate(((wq, oq), (wk, ok))):
        full = jnp.dot(hb, w_ref[...], preferred_element_type=F32)
        for g in range(D_ATTN // LANES):
            cols = slice(g * LANES, (g + 1) * LANES)
            t = full[:, cols]
            o_ref[:, cols] = (t * _pair_rms(t, lo) * gqk_ref[row:row + 1, cols]).astype(o_ref.dtype)
    ovt[0] = lax.dot_general(wvt[...], hb, (((1,), (1,)), ((), ())),
                             preferred_element_type=F32).astype(ovt.dtype)


def _inproj_call(x2, mod4, gain, gqk, weights, bsz, seq):
    rows = x2.shape[0]
    tiles_per_batch = seq // TM_PROJ
    w_z, w_xbc, w_g, w_q, w_k, w_vt, w_za = weights
    row_spec = lambda n: pl.BlockSpec((TM_PROJ, n), lambda i: (i, 0))
    row_out = lambda n, dt: jax.ShapeDtypeStruct((rows, n), dt)
    return pl.pallas_call(
        _inproj_kernel,
        grid=(rows // TM_PROJ,),
        in_specs=[row_spec(D_MODEL),
                  pl.BlockSpec((3, 1, 1, D_MODEL), lambda i: (0, i // tiles_per_batch, 0, 0)),
                  _const_spec((1, D_MODEL)), _const_spec(gqk.shape)]
                 + [_const_spec(w.shape) for w in weights],
        out_specs=[row_spec(D_SSM), row_spec(CONV_DIM), row_spec(GATE_COLS), row_spec(D_ATTN),
                   row_spec(D_ATTN),
                   pl.BlockSpec((1, D_ATTN, TM_PROJ),
                                lambda i: (i // tiles_per_batch, 0, i % tiles_per_batch)),
                   row_spec(D_ATTN)],
        out_shape=[row_out(D_SSM, BF16), row_out(CONV_DIM, BF16), row_out(GATE_COLS, F32),
                   row_out(D_ATTN, BF16), row_out(D_ATTN, BF16),
                   jax.ShapeDtypeStruct((bsz, D_ATTN, seq), BF16), row_out(D_ATTN, BF16)],
        compiler_params=pltpu.CompilerParams(dimension_semantics=("arbitrary",),
                                             vmem_limit_bytes=VMEM_LIMIT),
        name="in_proj",
    )(x2, mod4, gain, gqk, *weights)


def _gates_kernel(g_ref, plan_ref, nat_ref, gt_ref, faug_ref, ftab_ref, carry_ref):
    @pl.when(pl.program_id(1) == 0)
    def _():
        carry_ref[...] = jnp.zeros_like(carry_ref)

    g = g_ref[0] + plan_ref[0:1, :]
    soft = jnp.log1p(jnp.exp(-jnp.abs(g)))
    dt = jnp.maximum(g, 0.0) + soft
    logf = jnp.minimum(g, 0.0) - soft
    lane = lax.broadcasted_iota(jnp.int32, (1, LANES), 1)
    is_dt = lane < SSM_HEADS
    is_f = (lane >= F_LANE0) & (lane < F_LANE0 + F_SPLIT * ATTN_HEADS)
    src = jnp.where(is_dt, 0.0, jnp.where(lane < F_LANE0, dt * plan_ref[1:2, :],
                                          jnp.where(is_f, logf, 0.0)))
    r_i = lax.broadcasted_iota(jnp.int32, (CHUNK, CHUNK), 0)
    c_i = lax.broadcasted_iota(jnp.int32, (CHUNK, CHUNK), 1)
    tri = (r_i >= c_i).astype(F32)
    carry = carry_ref[...]
    chunks = []
    for ch in range(TG // CHUNK):
        rows = slice(ch * CHUNK, (ch + 1) * CHUNK)
        pref = jnp.dot(tri, src[rows], precision=lax.Precision.HIGHEST, preferred_element_type=F32)
        res = jnp.where(is_dt, dt[rows], pref + jnp.where(is_f, carry, 0.0))
        carry = res[CHUNK - 1:CHUNK, :]
        nat_ref[0, rows, :] = res
        gt_ref[0, :, rows] = res.T[0:2 * SSM_HEADS]
        chunks.append(res)
    carry_ref[...] = carry

    fcum = jnp.concatenate(chunks, axis=0)
    first = fcum[0:1, :]
    ftab_ref[0, 0] = first
    d = (first - fcum) * LOG2E
    hi = d.astype(BF16).astype(F32)
    r1 = d - hi
    mid = r1.astype(BF16).astype(F32)
    lo = (r1 - mid).astype(BF16).astype(F32)
    faug_ref[0] = (plan_ref[2:3, :] * hi + plan_ref[3:4, :] * mid + plan_ref[4:5, :] * lo).astype(BF16)


def _gates_call(g3, plan):
    bsz, seq, _ = g3.shape
    return pl.pallas_call(
        _gates_kernel,
        grid=(bsz, seq // TG),
        in_specs=[pl.BlockSpec((1, TG, GATE_COLS), lambda b, i: (b, i, 0)),
                  pl.BlockSpec((8, GATE_COLS), lambda b, i: (0, 0))],
        out_specs=[pl.BlockSpec((1, TG, GATE_COLS), lambda b, i: (b, i, 0)),
                   pl.BlockSpec((1, 2 * SSM_HEADS, TG), lambda b, i: (b, 0, i)),
                   pl.BlockSpec((1, TG, GATE_COLS), lambda b, i: (b, i, 0)),
                   pl.BlockSpec((1, 1, 1, GATE_COLS), lambda b, i: (b, i, 0, 0))],
        out_shape=[jax.ShapeDtypeStruct((bsz, seq, GATE_COLS), F32),
                   jax.ShapeDtypeStruct((bsz, 2 * SSM_HEADS, seq), F32),
                   jax.ShapeDtypeStruct((bsz, seq, GATE_COLS), BF16),
                   jax.ShapeDtypeStruct((bsz, seq // TG, 1, GATE_COLS), F32)],
        scratch_shapes=[pltpu.VMEM((1, GATE_COLS), F32)],
        compiler_params=pltpu.CompilerParams(dimension_semantics=("arbitrary", "arbitrary")),
        name="gate_prefix",
    )(g3, plan)


HALO = 8


def _ssd_kernel(xbc_ref, z_ref, nat_ref, gt_ref, cw_ref, cb_ref, dskip_ref, gain_ref,
                y_ref, ubuf, state, ybuf):
    @pl.when(pl.program_id(1) == 0)
    def _():
        ubuf[0:HALO, :] = jnp.zeros((HALO, CONV_DIM), F32)
        state[...] = jnp.zeros_like(state)

    cur = xbc_ref[0].astype(F32)
    ubuf[HALO:HALO + CHUNK, :] = cur
    conv = cw_ref[CONV_K - 1:CONV_K, :] * cur + cb_ref[...]
    for k in range(CONV_K - 1):
        off = HALO - (CONV_K - 1) + k
        conv = conv + cw_ref[k:k + 1, :] * ubuf[off:off + CHUNK, :]
    ubuf[0:HALO, :] = cur[CHUNK - HALO:CHUNK, :]
    xbc = _silu(conv)

    nat = nat_ref[0]
    gt = gt_ref[0]
    lane = lax.broadcasted_iota(jnp.int32, (1, LANES), 1)
    lo = lane < SSM_HEAD_DIM
    row_i = lax.broadcasted_iota(jnp.int32, (CHUNK, CHUNK), 0)
    col_i = lax.broadcasted_iota(jnp.int32, (CHUNK, CHUNK), 1)
    causal = row_i >= col_i
    lo_rows = row_i < SSM_HEAD_DIM
    contract_last = (((1,), (1,)), ((), ()))
    contract_first = (((0,), (0,)), ((), ()))

    for grp in range(SSM_GROUPS):
        b_g = xbc[:, D_SSM + grp * SSM_STATE:D_SSM + (grp + 1) * SSM_STATE].astype(BF16)
        c_g = xbc[:, D_SSM + D_BC + grp * SSM_STATE:D_SSM + D_BC + (grp + 1) * SSM_STATE].astype(BF16)
        cb = lax.dot_general(c_g, b_g, contract_last, preferred_element_type=F32)
        for pr in range(SSM_HEADS // SSM_GROUPS // 2):
            pair = grp * (SSM_HEADS // SSM_GROUPS // 2) + pr
            xp = xbc[:, pair * LANES:(pair + 1) * LANES]
            masks, ecol, dte, cdec = [], [], [], []
            for hd in (2 * pair, 2 * pair + 1):
                cum_col = nat[:, SSM_HEADS + hd:SSM_HEADS + hd + 1]
                dt_col = nat[:, hd:hd + 1]
                cum_row = gt[SSM_HEADS + hd:SSM_HEADS + hd + 1, :]
                dt_row = gt[hd:hd + 1, :]
                cum_last = cum_col[CHUNK - 1:CHUNK, :]
                decay = jnp.where(causal, jnp.exp(cum_col - cum_row), 0.0)
                masks.append((cb * decay * dt_row).astype(BF16))
                ecol.append(jnp.exp(cum_col))
                dte.append(dt_col * jnp.exp(cum_last - cum_col))
                cdec.append(jnp.exp(cum_last))
            yd = jnp.dot(jnp.concatenate(masks, axis=0), xp.astype(BF16),
                         preferred_element_type=F32)
            y_diag = jnp.where(lo, yd[:CHUNK], yd[CHUNK:])
            st = state[pair]
            y_off = lax.dot_general(c_g, st.astype(BF16), contract_last, preferred_element_type=F32)
            y_off = y_off * jnp.where(lo, ecol[0], ecol[1])
            xs = (xp * jnp.where(lo, dte[0], dte[1])).astype(BF16)
            upd = lax.dot_general(xs, b_g, contract_first, preferred_element_type=F32)
            state[pair] = st * jnp.where(lo_rows, cdec[0], cdec[1]) + upd
            ybuf[:, pair * LANES:(pair + 1) * LANES] = (
                y_diag + y_off + xp * dskip_ref[:, pair * LANES:(pair + 1) * LANES])

    gated = ybuf[...] * _silu(z_ref[0].astype(F32))
    width = D_SSM // SSM_GROUPS
    for grp in range(SSM_GROUPS):
        yg = gated[:, grp * width:(grp + 1) * width]
        ms = jnp.mean(yg * yg, axis=-1, keepdims=True)
        y_ref[0, :, grp * width:(grp + 1) * width] = (
            yg * lax.rsqrt(ms + EPS) * gain_ref[:, grp * width:(grp + 1) * width]).astype(y_ref.dtype)


def _ssd_call(xbc3, z3, nat, gt, conv_w, conv_b, dskip_row, gain_row):
    bsz, seq, _ = xbc3.shape
    return pl.pallas_call(
        _ssd_kernel,
        grid=(bsz, seq // CHUNK),
        in_specs=[pl.BlockSpec((1, CHUNK, CONV_DIM), lambda b, i: (b, i, 0)),
                  pl.BlockSpec((1, CHUNK, D_SSM), lambda b, i: (b, i, 0)),
                  pl.BlockSpec((1, CHUNK, GATE_COLS), lambda b, i: (b, i, 0)),
                  pl.BlockSpec((1, 2 * SSM_HEADS, CHUNK), lambda b, i: (b, 0, i)),
                  pl.BlockSpec((CONV_K, CONV_DIM), lambda b, i: (0, 0)),
                  pl.BlockSpec((1, CONV_DIM), lambda b, i: (0, 0)),
                  pl.BlockSpec((1, D_SSM), lambda b, i: (0, 0)),
                  pl.BlockSpec((1, D_SSM), lambda b, i: (0, 0))],
        out_specs=pl.BlockSpec((1, CHUNK, D_SSM), lambda b, i: (b, i, 0)),
        out_shape=jax.ShapeDtypeStruct((bsz, seq, D_SSM), BF16),
        scratch_shapes=[pltpu.VMEM((HALO + CHUNK, CONV_DIM), F32),
                        pltpu.VMEM((SSM_HEADS // 2, LANES, SSM_STATE), F32),
                        pltpu.VMEM((CHUNK, D_SSM), F32)],
        compiler_params=pltpu.CompilerParams(dimension_semantics=("arbitrary", "arbitrary")),
        name="ssd_heads",
    )(xbc3, z3, nat, gt, conv_w, conv_b, dskip_row, gain_row)


def _attn_kernel(ftab_ref, q_ref, k_ref, vt_ref, faug_ref, z_ref, go_ref,
                 o_ref, kaug_ref, vaug_ref, qs_ref, s0_ref, s1_ref, cmax0_ref, cmax1_ref,
                 acc0_ref, acc1_ref, *, seq):
    b = pl.program_id(0)
    hp = pl.program_id(1)
    nblk = seq // TK
    nq = seq // TQ
    n_qt = 2 * TQ // QT
    s_refs, cmax_refs, acc_refs = (s0_ref, s1_ref), (cmax0_ref, cmax1_ref), (acc0_ref, acc1_ref)
    lane = lax.broadcasted_iota(jnp.int32, (1, LANES), 1)
    lo = lane < ATTN_HEAD_DIM
    contract_last = (((1,), (1,)), ((), ()))

    def prep(i, carry):
        rows = pl.ds(pl.multiple_of(i * K_PREP_ROWS, K_PREP_ROWS), K_PREP_ROWS)
        kaug_ref[rows, 0:LANES] = k_ref[0, rows, :]
        kaug_ref[rows, LANES:2 * LANES] = faug_ref[0, rows, :]
        vaug_ref[0:LANES, rows] = vt_ref[0, :, rows]
        return carry
    lax.fori_loop(0, seq // K_PREP_ROWS, prep, 0)
    vaug_ref[LANES:V_ROWS, :] = jnp.ones((V_ROWS - LANES, seq), BF16)

    def build_qs(qi, qslot):
        rows = pl.ds(pl.multiple_of(qi * TQ, TQ), TQ)
        qn = q_ref[0, rows, :]
        for half in range(2):
            f0 = F_LANE0 + F_SPLIT * (2 * hp + half)
            ones = jnp.where((lane >= f0) & (lane < f0 + F_SPLIT), 1.0, 0.0)
            keep = lo if half == 0 else jnp.logical_not(lo)
            qs_ref[qslot, half * TQ:(half + 1) * TQ, 0:LANES] = jnp.where(keep, qn, jnp.zeros_like(qn))
            qs_ref[qslot, half * TQ:(half + 1) * TQ, LANES:2 * LANES] = (
                jnp.broadcast_to(ones, (TQ, LANES)).astype(BF16))

    tab0 = (b * ATTN_HEADS + 2 * hp) * nblk

    def tile_offset(qi, j, qt):
        tab = tab0 + (qt * QT // TQ) * nblk
        base = jnp.full((1, QT), ftab_ref[tab + qi * KT_PER_QT])
        return (base - jnp.full((1, QT), ftab_ref[tab + j])) * LOG2E

    def fully_masked(qt, diag):
        return diag is not None and (qt * QT) % TQ + QT - 1 < diag * TK

    def a_tile(qi, j, qslot, slot, diag, qt):
        start = pl.multiple_of(j * TK, TK)
        cols = slice(qt * QT, (qt + 1) * QT)
        st = lax.dot_general(kaug_ref[pl.ds(start, TK), :], qs_ref[qslot, cols, :], contract_last,
                             preferred_element_type=F32)
        q_first = (qt * QT) % TQ
        if diag is not None and q_first < diag * TK + TK - 1:
            k_i = lax.broadcasted_iota(jnp.int32, (TK, QT), 0) + diag * TK
            q_i = lax.broadcasted_iota(jnp.int32, (TK, QT), 1) + q_first
            st = jnp.where(k_i <= q_i, st, NEG_BIG)
        s_refs[slot][:, cols] = st
        cmax_refs[slot][:, cols] = jnp.max(st, axis=0, keepdims=True)

    def bc_tile(qi, j, slot, acc_ref, first, qt, m):
        start = pl.multiple_of(j * TK, TK)
        cols = slice(qt * QT, (qt + 1) * QT)
        c = tile_offset(qi, j, qt)
        m_here = cmax_refs[slot][:, cols] + c
        m_new = m_here if first else jnp.maximum(m, m_here)
        p = jnp.exp2(s_refs[slot][:, cols] - (m_new - c)).astype(BF16)
        pv = jnp.dot(vaug_ref[:, pl.ds(start, TK)], p, preferred_element_type=F32)
        if first:
            acc_ref[:, cols] = pv
        else:
            acc_ref[:, cols] = jnp.exp2(m - m_new) * acc_ref[:, cols] + pv
        return m_new

    def half_region(a_step, bc_step, ms):
        ms = list(ms)
        for qt in range(n_qt):
            if a_step is not None and not fully_masked(qt, a_step[4]):
                qi, j, qslot, slot, diag = a_step
                a_tile(qi, j, qslot, slot, diag, qt)
            if bc_step is not None and not fully_masked(qt, bc_step[4]):
                qi, j, slot, acc_ref, _, first = bc_step
                ms[qt] = bc_tile(qi, j, slot, acc_ref, first, qt, ms[qt])
        return tuple(ms)

    def finalize(qi, acc_ref):
        rows = pl.ds(pl.multiple_of(qi * TQ, TQ), TQ)
        heads = []
        for half in range(2):
            ch = slice(half * ATTN_HEAD_DIM, (half + 1) * ATTN_HEAD_DIM)
            qs = slice(half * TQ, (half + 1) * TQ)
            o_h = acc_ref[ch, qs] * (1.0 / acc_ref[LANES:LANES + 1, qs])
            ms = jnp.mean(o_h * o_h, axis=0, keepdims=True)
            heads.append(o_h * lax.rsqrt(ms + EPS))
        o = jnp.concatenate(heads, axis=0).T * go_ref[...]
        o_ref[0, rows, :] = (o * _silu(z_ref[0, rows, :].astype(F32))).astype(o_ref.dtype)

    last_d = KT_PER_QT - 1

    def query_tile(qi, par, ms):
        acc, acc_prev = acc_refs[par], acc_refs[1 - par]
        n_off = qi * KT_PER_QT
        build_qs(qi, par)
        ms = half_region((qi, 0, par, 0, None), (qi - 1, n_off - 1, 1, acc_prev, last_d, False), ms)
        ms = half_region((qi, 1, par, 1, None), (qi, 0, 0, acc, None, True), ms)
        finalize(qi - 1, acc_prev)

        def pair(i, ms):
            ms = half_region((qi, 2 * i, par, 0, None), (qi, 2 * i - 1, 1, acc, None, False), ms)
            return half_region((qi, 2 * i + 1, par, 1, None), (qi, 2 * i, 0, acc, None, False), ms)
        ms = lax.fori_loop(1, qi, pair, ms)
        ms = half_region((qi, n_off, par, 0, 0), (qi, n_off - 1, 1, acc, None, False), ms)
        return half_region((qi, n_off + 1, par, 1, 1), (qi, n_off, 0, acc, 0, False), ms)

    ms = tuple(jnp.full((1, QT), NEG_BIG, F32) for _ in range(n_qt))
    build_qs(0, 0)
    ms = half_region((0, 0, 0, 0, 0), None, ms)
    ms = half_region((0, 1, 0, 1, 1), (0, 0, 0, acc0_ref, 0, True), ms)

    def two_tiles(u, ms):
        qi = 2 * u + 1
        ms = query_tile(qi, 1, ms)
        return lax.cond(qi + 1 < nq, lambda v: query_tile(qi + 1, 0, v), lambda v: v, ms)
    ms = lax.fori_loop(0, nq // 2, two_tiles, ms)
    acc_last = acc_refs[(nq - 1) % 2]
    half_region(None, (nq - 1, nq * KT_PER_QT - 1, 1, acc_last, last_d, False), ms)
    finalize(nq - 1, acc_last)


def _attn_call(ftab, q3, k3, vt3, faug, z3, go_row):
    bsz, seq, _ = q3.shape
    rows_spec = lambda imap: pl.BlockSpec((1, seq, LANES), imap)
    return pl.pallas_call(
        functools.partial(_attn_kernel, seq=seq),
        grid=(bsz, HEAD_PAIRS),
        in_specs=[pl.BlockSpec(memory_space=pltpu.SMEM),
                  rows_spec(lambda b, h: (b, 0, h)),
                  rows_spec(lambda b, h: (b, 0, h)),
                  pl.BlockSpec((1, LANES, seq), lambda b, h: (b, h, 0)),
                  rows_spec(lambda b, h: (b, 0, 0)),
                  rows_spec(lambda b, h: (b, 0, h)),
                  pl.BlockSpec((1, LANES), lambda b, h: (0, h))],
        out_specs=rows_spec(lambda b, h: (b, 0, h)),
        out_shape=jax.ShapeDtypeStruct((bsz, seq, D_ATTN), BF16),
        scratch_shapes=[pltpu.VMEM((seq, 2 * LANES), BF16),
                        pltpu.VMEM((V_ROWS, seq), BF16),
                        pltpu.VMEM((2, 2 * TQ, 2 * LANES), BF16),
                        pltpu.VMEM((TK, 2 * TQ), F32), pltpu.VMEM((TK, 2 * TQ), F32),
                        pltpu.VMEM((1, 2 * TQ), F32), pltpu.VMEM((1, 2 * TQ), F32),
                        pltpu.VMEM((V_ROWS, 2 * TQ), F32), pltpu.VMEM((V_ROWS, 2 * TQ), F32)],
        compiler_params=pltpu.CompilerParams(
            dimension_semantics=("arbitrary", "arbitrary"),
            vmem_limit_bytes=VMEM_LIMIT),
        name="fox_attention",
    )(ftab, q3, k3, vt3, faug, z3, go_row)


def _outproj_kernel(x_ref, y_ref, o_ref, mod_ref, wy_ref, wo_ref, out_ref):
    mixed = jnp.dot(y_ref[...], wy_ref[...], preferred_element_type=F32)
    mixed = mixed + jnp.dot(o_ref[...], wo_ref[...], preferred_element_type=F32)
    out_ref[...] = x_ref[...] + mod_ref[2, 0] * mixed


def _outproj_call(x2, y2, o2, mod4, wy, wo, seq):
    rows = x2.shape[0]
    tiles_per_batch = seq // TM_PROJ
    row_spec = lambda n: pl.BlockSpec((TM_PROJ, n), lambda i: (i, 0))
    return pl.pallas_call(
        _outproj_kernel,
        grid=(rows // TM_PROJ,),
        in_specs=[row_spec(D_MODEL), row_spec(D_SSM), row_spec(D_ATTN),
                  pl.BlockSpec((3, 1, 1, D_MODEL), lambda i: (0, i // tiles_per_batch, 0, 0)),
                  _const_spec(wy.shape), _const_spec(wo.shape)],
        out_specs=row_spec(D_MODEL),
        out_shape=jax.ShapeDtypeStruct((rows, D_MODEL), F32),
        compiler_params=pltpu.CompilerParams(dimension_semantics=("arbitrary",),
                                             vmem_limit_bytes=VMEM_LIMIT),
        name="out_proj",
    )(x2, y2, o2, mod4, wy, wo)


def _pad_cols(w, width):
    return jnp.pad(w, ((0, 0), (0, width - w.shape[1])))


def _layer(x, c, norm_gain, w_ada, b_ada, w_in, conv_w, conv_b, dt_bias, a_log, d_skip,
           ssm_norm_gain, q_norm_gain, k_norm_gain, forget_bias, attn_norm_gain, w_out):
    bsz, seq, _ = x.shape
    assert seq % TM_PROJ == 0 and seq % TG == 0 and seq % TQ == 0 and seq % K_PREP_ROWS == 0
    assert KT_PER_QT == 2 and TQ % QT == 0 and TK % QT == 0
    x2 = x.reshape(bsz * seq, D_MODEL)

    mod4 = _ada_call(c, w_ada, b_ada).reshape(3, bsz, 1, D_MODEL)

    o_x = D_SSM
    o_dt = o_x + CONV_DIM
    o_q = o_dt + SSM_HEADS
    o_f = o_q + 4 * D_ATTN
    w_dt = w_in[:, o_dt:o_dt + SSM_HEADS]
    w_f3 = jnp.repeat(w_in[:, o_f:o_f + ATTN_HEADS], F_SPLIT, axis=1)
    weights = [w_in[:, :D_SSM], w_in[:, o_x:o_x + CONV_DIM],
               _pad_cols(jnp.concatenate([w_dt, w_dt, w_f3], axis=1), GATE_COLS),
               w_in[:, o_q:o_q + D_ATTN], w_in[:, o_q + D_ATTN:o_q + 2 * D_ATTN],
               w_in[:, o_q + 2 * D_ATTN:o_q + 3 * D_ATTN].T, w_in[:, o_q + 3 * D_ATTN:o_q + 4 * D_ATTN]]
    weights = [w.astype(BF16) for w in weights]
    gqk = jnp.stack([jnp.tile(q_norm_gain, ATTN_HEADS) * (LOG2E / math.sqrt(ATTN_HEAD_DIM)),
                     jnp.tile(k_norm_gain, ATTN_HEADS)]).astype(F32)
    z_ssd, xbc, graw, q, k, vt, z_attn = _inproj_call(
        x2, mod4, norm_gain.reshape(1, D_MODEL), gqk, weights, bsz, seq)

    a_neg = -jnp.exp(a_log.astype(F32))
    f_lane = jnp.arange(F_SPLIT * ATTN_HEADS) % F_SPLIT
    plan_rows = [jnp.concatenate([dt_bias, dt_bias, jnp.repeat(forget_bias, F_SPLIT)]),
                 jnp.concatenate([jnp.zeros_like(a_neg), a_neg])]
    plan_rows += [jnp.concatenate([jnp.zeros((F_LANE0,), F32), (f_lane == t).astype(F32)])
                  for t in range(F_SPLIT)]
    plan = jnp.stack([jnp.pad(r.astype(F32), (0, GATE_COLS - r.shape[0])) for r in plan_rows])
    plan = jnp.pad(plan, ((0, 8 - plan.shape[0]), (0, 0)))
    nat, gt, faug, ftab = _gates_call(graw.reshape(bsz, seq, GATE_COLS), plan)

    to3 = lambda t: t.reshape(bsz, seq, t.shape[-1])
    y = _ssd_call(to3(xbc), to3(z_ssd), nat, gt, conv_w, conv_b.reshape(1, CONV_DIM),
                  jnp.repeat(d_skip, SSM_HEAD_DIM).reshape(1, D_SSM),
                  ssm_norm_gain.reshape(1, D_SSM))

    ftab_flat = jnp.transpose(ftab[:, :, 0, F_LANE0:F_LANE0 + F_SPLIT * ATTN_HEADS:F_SPLIT],
                              (0, 2, 1)).reshape(-1)
    o = _attn_call(ftab_flat, to3(q), to3(k), vt, faug, to3(z_attn), attn_norm_gain.reshape(1, D_ATTN))

    out = _outproj_call(x2, y.reshape(bsz * seq, D_SSM), o.reshape(bsz * seq, D_ATTN), mod4,
                        w_out[:D_SSM].astype(BF16), w_out[D_SSM:].astype(BF16), seq)
    return out.reshape(bsz, seq, D_MODEL)


def kernel(x, c, norm_gain, w_ada, b_ada, w_in, conv_w, conv_b, dt_bias, a_log, d_skip,
           ssm_norm_gain, q_norm_gain, k_norm_gain, forget_bias, attn_norm_gain, w_out):
    for layer in range(norm_gain.shape[0]):
        x = _layer(x, c, norm_gain[layer], w_ada[layer], b_ada[layer], w_in[layer],
                   conv_w[layer], conv_b[layer], dt_bias[layer], a_log[layer], d_skip[layer],
                   ssm_norm_gain[layer], q_norm_gain[layer], k_norm_gain[layer],
                   forget_bias[layer], attn_norm_gain[layer], w_out[layer])
    return x
```

```python
import functools
import math

import jax
import jax.numpy as jnp
from jax import lax
from jax.experimental import pallas as pl
from jax.experimental.pallas import tpu as pltpu

F32 = jnp.float32
BF16 = jnp.bfloat16

D_MODEL = 1024
D_SSM = 1024
D_ATTN = 1024
SSM_HEAD_DIM = 64
SSM_HEADS = 16
SSM_GROUPS = 2
SSM_STATE = 128
CONV_K = 4
CHUNK = 128
D_BC = SSM_GROUPS * SSM_STATE
CONV_DIM = D_SSM + 2 * D_BC
ATTN_HEAD_DIM = 64
ATTN_HEADS = 16
EPS = 1e-6
LANES = 128
HEAD_PAIRS = ATTN_HEADS // 2
GATE_COLS = LANES
F_LANE0 = 2 * SSM_HEADS
F_SPLIT = 3
LOG2E = math.log2(math.e)

TM_PROJ = 512
TG = 512
TQ = 1024
TK = TG
KT_PER_QT = TQ // TK
QT = 256
V_ROWS = LANES + 16
K_PREP_ROWS = 1024
NEG_BIG = -1e30
VMEM_LIMIT = 56 * 1024 * 1024


def _silu(v):
    return v * jax.nn.sigmoid(v)


def _const_spec(shape):
    nd = len(shape)
    return pl.BlockSpec(shape, lambda *_: (0,) * nd, pipeline_mode=pl.Buffered(1))


def _ada_kernel(c_ref, w_ref, b_ref, o_ref):
    c = c_ref[...]
    o_ref[0] = jnp.dot(_silu(c), w_ref[...], precision=lax.Precision.HIGHEST,
                       preferred_element_type=F32) + b_ref[0]


def _ada_call(c, w_ada, b_ada):
    bsz = c.shape[0]
    return pl.pallas_call(
        _ada_kernel,
        grid=(3,),
        in_specs=[pl.BlockSpec((bsz, D_MODEL), lambda j: (0, 0)),
                  pl.BlockSpec((D_MODEL, D_MODEL), lambda j: (0, j)),
                  pl.BlockSpec((1, 1, D_MODEL), lambda j: (j, 0, 0))],
        out_specs=pl.BlockSpec((1, bsz, D_MODEL), lambda j: (j, 0, 0)),
        out_shape=jax.ShapeDtypeStruct((3, bsz, D_MODEL), F32),
        name="adaln_mod",
    )(c, w_ada, b_ada.reshape(3, 1, D_MODEL))


CHUNKS_PER_TILE = TM_PROJ // CHUNK
PROJ_SLAB = 256


def _pair_rms(v, lo):
    sq = v * v
    s_lo = jnp.sum(jnp.where(lo, sq, 0.0), axis=-1, keepdims=True)
    s_hi = jnp.sum(jnp.where(lo, 0.0, sq), axis=-1, keepdims=True)
    return lax.rsqrt(jnp.where(lo, s_lo, s_hi) * (1.0 / ATTN_HEAD_DIM) + EPS)


def _conv_shift_matrix():
    t = jnp.arange(CHUNK)[:, None]
    r = jnp.arange(2 * CHUNK)[None, :]
    blocks = [(r == t + CHUNK - (CONV_K - 1) + k) for k in range(CONV_K - 1)]
    return jnp.concatenate(blocks, axis=0).astype(BF16)


def _gate_prefix(graw, plan_ref, carry_ref, nat_s, gt_s, faug_ref, ftab_ref):
    g = graw + plan_ref[0:1, :]
    soft = jnp.log1p(jnp.exp(-jnp.abs(g)))
    dt = jnp.maximum(g, 0.0) + soft
    logf = jnp.minimum(g, 0.0) - soft
    lane = lax.broadcasted_iota(jnp.int32, (1, LANES), 1)
    is_dt = lane < SSM_HEADS
    is_f = (lane >= F_LANE0) & (lane < F_LANE0 + F_SPLIT * ATTN_HEADS)
    src = jnp.where(is_dt, 0.0, jnp.where(lane < F_LANE0, dt * plan_ref[1:2, :],
                                          jnp.where(is_f, logf, 0.0)))
    r_i = lax.broadcasted_iota(jnp.int32, (CHUNK, CHUNK), 0)
    c_i = lax.broadcasted_iota(jnp.int32, (CHUNK, CHUNK), 1)
    tri = (r_i >= c_i).astype(F32)
    carry = carry_ref[...]
    chunks = []
    for ch in range(CHUNKS_PER_TILE):
        rows = slice(ch * CHUNK, (ch + 1) * CHUNK)
        pref = jnp.dot(tri, src[rows], precision=lax.Precision.HIGHEST, preferred_element_type=F32)
        res = jnp.where(is_dt, dt[rows], pref + jnp.where(is_f, carry, 0.0))
        carry = res[CHUNK - 1:CHUNK, :]
        nat_s[rows, :] = res
        gt_s[:, rows] = res.T[0:2 * SSM_HEADS]
        chunks.append(res)
    carry_ref[...] = carry

    fcum = jnp.concatenate(chunks, axis=0)
    first = fcum[0:1, :]
    ftab_ref[0] = first
    d = (first - fcum) * LOG2E
    hi = d.astype(BF16).astype(F32)
    r1 = d - hi
    mid = r1.astype(BF16).astype(F32)
    lo = (r1 - mid).astype(BF16).astype(F32)
    faug_ref[...] = (plan_ref[2:3, :] * hi + plan_ref[3:4, :] * mid + plan_ref[4:5, :] * lo).astype(BF16)


def _ssd_chunk(sub, prev_tail, ubuf, z_s, nat_s, gt_s, shift_ref, cw_ref, cb_ref, dskip_ref, gain_ref,
               y_ref, state, ybuf, fillers):
    fillers = list(fillers)
    lane = lax.broadcasted_iota(jnp.int32, (1, LANES), 1)
    lo = lane < SSM_HEAD_DIM
    row_i = lax.broadcasted_iota(jnp.int32, (CHUNK, CHUNK), 0)
    col_i = lax.broadcasted_iota(jnp.int32, (CHUNK, CHUNK), 1)
    causal = row_i >= col_i
    lo_rows = row_i < SSM_HEAD_DIM
    contract_last = (((1,), (1,)), ((), ()))
    contract_first = (((0,), (0,)), ((), ()))
    width = D_SSM // SSM_GROUPS
    rows = slice(sub * CHUNK, (sub + 1) * CHUNK)

    fillers.pop(0)()
    prev = prev_tail if sub == 0 else ubuf[(sub - 1) * CHUNK:sub * CHUNK, :]
    window = jnp.concatenate([prev, ubuf[rows, :]], axis=0)
    shifted = jnp.dot(shift_ref[...], window, preferred_element_type=F32)
    conv = cw_ref[CONV_K - 1:CONV_K, :] * window[CHUNK:, :].astype(F32) + cb_ref[...]
    for k in range(CONV_K - 1):
        conv = conv + cw_ref[k:k + 1, :] * shifted[k * CHUNK:(k + 1) * CHUNK, :]
    xbc = _silu(conv)

    nat = nat_s[rows, :]
    gt = gt_s[:, rows]
    for grp in range(SSM_GROUPS):
        b_g = xbc[:, D_SSM + grp * SSM_STATE:D_SSM + (grp + 1) * SSM_STATE].astype(BF16)
        c_g = xbc[:, D_SSM + D_BC + grp * SSM_STATE:D_SSM + D_BC + (grp + 1) * SSM_STATE].astype(BF16)
        cb = lax.dot_general(c_g, b_g, contract_last, preferred_element_type=F32)
        for pr in range(SSM_HEADS // SSM_GROUPS // 2):
            pair = grp * (SSM_HEADS // SSM_GROUPS // 2) + pr
            xp = xbc[:, pair * LANES:(pair + 1) * LANES]
            masks, ecol, dte, cdec = [], [], [], []
            for hd in (2 * pair, 2 * pair + 1):
                cum_col = nat[:, SSM_HEADS + hd:SSM_HEADS + hd + 1]
                dt_col = nat[:, hd:hd + 1]
                cum_row = gt[SSM_HEADS + hd:SSM_HEADS + hd + 1, :]
                dt_row = gt[hd:hd + 1, :]
                cum_last = cum_col[CHUNK - 1:CHUNK, :]
                decay = jnp.where(causal, jnp.exp2(cum_col - cum_row), 0.0)
                masks.append((cb * decay * dt_row).astype(BF16))
                ecol.append(jnp.exp2(cum_col))
                dte.append(dt_col * jnp.exp2(cum_last - cum_col))
                cdec.append(jnp.exp2(cum_last))
            yd = jnp.dot(jnp.concatenate(masks, axis=0), xp.astype(BF16),
                         preferred_element_type=F32)
            y_diag = jnp.where(lo, yd[:CHUNK], yd[CHUNK:])
            st = state[pair]
            y_off = lax.dot_general(c_g, st.astype(BF16), contract_last, preferred_element_type=F32)
            y_off = y_off * jnp.where(lo, ecol[0], ecol[1])
            xs = (xp * jnp.where(lo, dte[0], dte[1])).astype(BF16)
            upd = lax.dot_general(xs, b_g, contract_first, preferred_element_type=F32)
            state[pair] = st * jnp.where(lo_rows, cdec[0], cdec[1]) + upd
            ybuf[rows, pair * LANES:(pair + 1) * LANES] = (
                y_diag + y_off + xp * dskip_ref[:, pair * LANES:(pair + 1) * LANES])
            if pair % 2 == 1 and fillers:
                fillers.pop(0)()

    while fillers:
        fillers.pop(0)()
    gated = ybuf[rows, :] * _silu(z_s[rows, :].astype(F32))
    for grp in range(SSM_GROUPS):
        yg = gated[:, grp * width:(grp + 1) * width]
        ms = jnp.mean(yg * yg, axis=-1, keepdims=True)
        y_ref[rows, grp * width:(grp + 1) * width] = (
            yg * lax.rsqrt(ms + EPS) * gain_ref[:, grp * width:(grp + 1) * width]).astype(y_ref.dtype)


def _proj_ssd_kernel(x_ref, mod_ref, gain_ref, gqk_ref, plan_ref, shift_ref, cw_ref, cb_ref,
                     dskip_ref, ygain_ref, wz, wxbc, wg, wq, wk, wvt, wza,
                     oy, oq, ok, ovt, oza, ofaug, oftab,
                     carry_ref, ubuf2, z_s, nat_s, gt_s, state, ybuf, *, tiles_per_batch):
    @pl.when(pl.program_id(0) % tiles_per_batch == 0)
    def _():
        carry_ref[...] = jnp.zeros_like(carry_ref)
        state[...] = jnp.zeros_like(state)

    x = x_ref[...]
    ms = jnp.mean(x * x, axis=-1, keepdims=True)
    shift = mod_ref[0, 0]
    scale = mod_ref[1, 0]
    h = (x * lax.rsqrt(ms + EPS) * gain_ref[...]) * (1.0 + scale) + shift
    hb = h.astype(BF16)

    slot = pl.program_id(0) % 2
    ubuf = ubuf2.at[slot]
    ubuf[...] = jnp.dot(hb, wxbc[...], preferred_element_type=F32).astype(BF16)
    first_tile = pl.program_id(0) % tiles_per_batch == 0
    prev_tail = ubuf2[1 - slot, TM_PROJ - CHUNK:TM_PROJ, :]
    prev_tail = jnp.where(first_tile, jnp.zeros_like(prev_tail), prev_tail)
    z_s[...] = jnp.dot(hb, wz[...], preferred_element_type=F32).astype(BF16)
    _gate_prefix(jnp.dot(hb, wg[...], preferred_element_type=F32), plan_ref, carry_ref,
                 nat_s, gt_s, ofaug, oftab)

    lo = lax.broadcasted_iota(jnp.int32, (1, LANES), 1) < ATTN_HEAD_DIM

    def normed_heads(row, w_ref, o_ref, slab):
        full = jnp.dot(hb, w_ref[:, slab], preferred_element_type=F32)
        for g in range(PROJ_SLAB // LANES):
            cols = slice(slab.start + g * LANES, slab.start + (g + 1) * LANES)
            t = full[:, g * LANES:(g + 1) * LANES]
            o_ref[:, cols] = (t * _pair_rms(t, lo) * gqk_ref[row:row + 1, cols]).astype(o_ref.dtype)

    def v_transposed(slab):
        ovt[0, slab, :] = lax.dot_general(wvt[slab, :], hb, (((1,), (1,)), ((), ())),
                                          preferred_element_type=F32).astype(ovt.dtype)

    def z_attn(slab):
        oza[:, slab] = jnp.dot(hb, wza[:, slab], preferred_element_type=F32).astype(oza.dtype)

    slabs = [slice(s * PROJ_SLAB, (s + 1) * PROJ_SLAB) for s in range(D_ATTN // PROJ_SLAB)]
    pieces = ([functools.partial(normed_heads, 0, wq, oq, s) for s in slabs]
              + [functools.partial(normed_heads, 1, wk, ok, s) for s in slabs]
              + [functools.partial(v_transposed, s) for s in slabs]
              + [functools.partial(z_attn, s) for s in slabs])
    per_chunk = len(pieces) // CHUNKS_PER_TILE
    for sub in range(CHUNKS_PER_TILE):
        _ssd_chunk(sub, prev_tail, ubuf, z_s, nat_s, gt_s, shift_ref, cw_ref, cb_ref, dskip_ref, ygain_ref,
                   oy, state, ybuf, pieces[sub * per_chunk:(sub + 1) * per_chunk])


def _proj_ssd_call(x2, mod4, gain, gqk, plan, shift, conv_w, conv_b, dskip_row, ygain_row, weights,
                   bsz, seq):
    rows = x2.shape[0]
    tiles_per_batch = seq // TM_PROJ
    assert CHUNKS_PER_TILE == 4
    row_spec = lambda n: pl.BlockSpec((TM_PROJ, n), lambda i: (i, 0))
    row_out = lambda n, dt: jax.ShapeDtypeStruct((rows, n), dt)
    consts = [gain, gqk, plan, shift, conv_w, conv_b, dskip_row, ygain_row] + list(weights)
    return pl.pallas_call(
        functools.partial(_proj_ssd_kernel, tiles_per_batch=tiles_per_batch),
        grid=(rows // TM_PROJ,),
        in_specs=[row_spec(D_MODEL),
                  pl.BlockSpec((3, 1, 1, D_MODEL), lambda i: (0, i // tiles_per_batch, 0, 0))]
                 + [_const_spec(c.shape) for c in consts],
        out_specs=[row_spec(D_SSM), row_spec(D_ATTN), row_spec(D_ATTN),
                   pl.BlockSpec((1, D_ATTN, TM_PROJ),
                                lambda i: (i // tiles_per_batch, 0, i % tiles_per_batch)),
                   row_spec(D_ATTN), row_spec(GATE_COLS),
                   pl.BlockSpec((1, 1, GATE_COLS), lambda i: (i, 0, 0))],
        out_shape=[row_out(D_SSM, BF16), row_out(D_ATTN, BF16), row_out(D_ATTN, BF16),
                   jax.ShapeDtypeStruct((bsz, D_ATTN, seq), BF16), row_out(D_ATTN, BF16),
                   row_out(GATE_COLS, BF16),
                   jax.ShapeDtypeStruct((rows // TM_PROJ, 1, GATE_COLS), F32)],
        scratch_shapes=[pltpu.VMEM((1, GATE_COLS), F32),
                        pltpu.VMEM((2, TM_PROJ, CONV_DIM), BF16),
                        pltpu.VMEM((TM_PROJ, D_SSM), BF16),
                        pltpu.VMEM((TM_PROJ, GATE_COLS), F32),
                        pltpu.VMEM((2 * SSM_HEADS, TM_PROJ), F32),
                        pltpu.VMEM((SSM_HEADS // 2, LANES, SSM_STATE), F32),
                        pltpu.VMEM((TM_PROJ, D_SSM), F32)],
        compiler_params=pltpu.CompilerParams(dimension_semantics=("arbitrary",),
                                             vmem_limit_bytes=VMEM_LIMIT),
        name="proj_ssd",
    )(x2, mod4, *consts)


def _attn_kernel(ftab_ref, q_ref, k_ref, vt_ref, faug_ref, z_ref, go_ref,
                 o_ref, kaug_ref, vaug_ref, qs_ref, s0_ref, s1_ref, cmax0_ref, cmax1_ref,
                 acc0_ref, acc1_ref, *, seq):
    b = pl.program_id(0)
    hp = pl.program_id(1)
    nblk = seq // TK
    nq = seq // TQ
    n_qt = 2 * TQ // QT
    s_refs, cmax_refs, acc_refs = (s0_ref, s1_ref), (cmax0_ref, cmax1_ref), (acc0_ref, acc1_ref)
    lane = lax.broadcasted_iota(jnp.int32, (1, LANES), 1)
    lo = lane < ATTN_HEAD_DIM
    contract_last = (((1,), (1,)), ((), ()))

    def prep(i, carry):
        rows = pl.ds(pl.multiple_of(i * K_PREP_ROWS, K_PREP_ROWS), K_PREP_ROWS)
        kaug_ref[rows, 0:LANES] = k_ref[0, rows, :]
        kaug_ref[rows, LANES:2 * LANES] = faug_ref[0, rows, :]
        vaug_ref[0:LANES, rows] = vt_ref[0, :, rows]
        return carry
    lax.fori_loop(0, seq // K_PREP_ROWS, prep, 0)
    vaug_ref[LANES:V_ROWS, :] = jnp.ones((V_ROWS - LANES, seq), BF16)

    def build_qs(qi, qslot):
        rows = pl.ds(pl.multiple_of(qi * TQ, TQ), TQ)
        qn = q_ref[0, rows, :]
        for half in range(2):
            f0 = F_LANE0 + F_SPLIT * (2 * hp + half)
            ones = jnp.where((lane >= f0) & (lane < f0 + F_SPLIT), 1.0, 0.0)
            keep = lo if half == 0 else jnp.logical_not(lo)
            qs_ref[qslot, half * TQ:(half + 1) * TQ, 0:LANES] = jnp.where(keep, qn, jnp.zeros_like(qn))
            qs_ref[qslot, half * TQ:(half + 1) * TQ, LANES:2 * LANES] = (
                jnp.broadcast_to(ones, (TQ, LANES)).astype(BF16))

    tab0 = (b * ATTN_HEADS + 2 * hp) * nblk

    def tile_offset(qi, j, qt):
        tab = tab0 + (qt * QT // TQ) * nblk
        base = jnp.full((1, QT), ftab_ref[tab + qi * KT_PER_QT])
        return (base - jnp.full((1, QT), ftab_ref[tab + j])) * LOG2E

    def fully_masked(qt, diag):
        return diag is not None and (qt * QT) % TQ + QT - 1 < diag * TK

    def a_tile(qi, j, qslot, slot, diag, qt):
        start = pl.multiple_of(j * TK, TK)
        cols = slice(qt * QT, (qt + 1) * QT)
        st = lax.dot_general(kaug_ref[pl.ds(start, TK), :], qs_ref[qslot, cols, :], contract_last,
                             preferred_element_type=F32)
        q_first = (qt * QT) % TQ
        if diag is not None and q_first < diag * TK + TK - 1:
            k_i = lax.broadcasted_iota(jnp.int32, (TK, QT), 0) + diag * TK
            q_i = lax.broadcasted_iota(jnp.int32, (TK, QT), 1) + q_first
            st = jnp.where(k_i <= q_i, st, NEG_BIG)
        s_refs[slot][:, cols] = st
        cmax_refs[slot][:, cols] = jnp.max(st, axis=0, keepdims=True)

    def bc_tile(qi, j, slot, acc_ref, first, qt, m):
        start = pl.multiple_of(j * TK, TK)
        cols = slice(qt * QT, (qt + 1) * QT)
        c = tile_offset(qi, j, qt)
        m_here = cmax_refs[slot][:, cols] + c
        m_new = m_here if first else jnp.maximum(m, m_here)
        p = jnp.exp2(s_refs[slot][:, cols] - (m_new - c)).astype(BF16)
        pv = jnp.dot(vaug_ref[:, pl.ds(start, TK)], p, preferred_element_type=F32)
        if first:
            acc_ref[:, cols] = pv
        else:
            acc_ref[:, cols] = jnp.exp2(m - m_new) * acc_ref[:, cols] + pv
        return m_new

    def half_region(a_step, bc_step, ms):
        ms = list(ms)
        for qt in range(n_qt):
            if a_step is not None and not fully_masked(qt, a_step[4]):
                qi, j, qslot, slot, diag = a_step
                a_tile(qi, j, qslot, slot, diag, qt)
            if bc_step is not None and not fully_masked(qt, bc_step[4]):
                qi, j, slot, acc_ref, _, first = bc_step
                ms[qt] = bc_tile(qi, j, slot, acc_ref, first, qt, ms[qt])
        return tuple(ms)

    def finalize(qi, acc_ref):
        rows = pl.ds(pl.multiple_of(qi * TQ, TQ), TQ)
        heads = []
        for half in range(2):
            ch = slice(half * ATTN_HEAD_DIM, (half + 1) * ATTN_HEAD_DIM)
            qs = slice(half * TQ, (half + 1) * TQ)
            o_h = acc_ref[ch, qs] * (1.0 / acc_ref[LANES:LANES + 1, qs])
            ms = jnp.mean(o_h * o_h, axis=0, keepdims=True)
            heads.append(o_h * lax.rsqrt(ms + EPS))
        o = jnp.concatenate(heads, axis=0).T * go_ref[...]
        o_ref[0, rows, :] = (o * _silu(z_ref[0, rows, :].astype(F32))).astype(o_ref.dtype)

    last_d = KT_PER_QT - 1

    def query_tile(qi, par, ms):
        acc, acc_prev = acc_refs[par], acc_refs[1 - par]
        n_off = qi * KT_PER_QT
        build_qs(qi, par)
        ms = half_region((qi, 0, par, 0, None), (qi - 1, n_off - 1, 1, acc_prev, last_d, False), ms)
        ms = half_region((qi, 1, par, 1, None), (qi, 0, 0, acc, None, True), ms)
        finalize(qi - 1, acc_prev)

        def pair(i, ms):
            ms = half_region((qi, 2 * i, par, 0, None), (qi, 2 * i - 1, 1, acc, None, False), ms)
            return half_region((qi, 2 * i + 1, par, 1, None), (qi, 2 * i, 0, acc, None, False), ms)
        ms = lax.fori_loop(1, qi, pair, ms)
        ms = half_region((qi, n_off, par, 0, 0), (qi, n_off - 1, 1, acc, None, False), ms)
        return half_region((qi, n_off + 1, par, 1, 1), (qi, n_off, 0, acc, 0, False), ms)

    ms = tuple(jnp.full((1, QT), NEG_BIG, F32) for _ in range(n_qt))
    build_qs(0, 0)
    ms = half_region((0, 0, 0, 0, 0), None, ms)
    ms = half_region((0, 1, 0, 1, 1), (0, 0, 0, acc0_ref, 0, True), ms)

    def two_tiles(u, ms):
        qi = 2 * u + 1
        ms = query_tile(qi, 1, ms)
        return lax.cond(qi + 1 < nq, lambda v: query_tile(qi + 1, 0, v), lambda v: v, ms)
    ms = lax.fori_loop(0, nq // 2, two_tiles, ms)
    acc_last = acc_refs[(nq - 1) % 2]
    half_region(None, (nq - 1, nq * KT_PER_QT - 1, 1, acc_last, last_d, False), ms)
    finalize(nq - 1, acc_last)


def _attn_call(ftab, q3, k3, vt3, faug, z3, go_row):
    bsz, seq, _ = q3.shape
    rows_spec = lambda imap: pl.BlockSpec((1, seq, LANES), imap)
    return pl.pallas_call(
        functools.partial(_attn_kernel, seq=seq),
        grid=(bsz, HEAD_PAIRS),
        in_specs=[pl.BlockSpec(memory_space=pltpu.SMEM),
                  rows_spec(lambda b, h: (b, 0, h)),
                  rows_spec(lambda b, h: (b, 0, h)),
                  pl.BlockSpec((1, LANES, seq), lambda b, h: (b, h, 0)),
                  rows_spec(lambda b, h: (b, 0, 0)),
                  rows_spec(lambda b, h: (b, 0, h)),
                  pl.BlockSpec((1, LANES), lambda b, h: (0, h))],
        out_specs=rows_spec(lambda b, h: (b, 0, h)),
        out_shape=jax.ShapeDtypeStruct((bsz, seq, D_ATTN), BF16),
        scratch_shapes=[pltpu.VMEM((seq, 2 * LANES), BF16),
                        pltpu.VMEM((V_ROWS, seq), BF16),
                        pltpu.VMEM((2, 2 * TQ, 2 * LANES), BF16),
                        pltpu.VMEM((TK, 2 * TQ), F32), pltpu.VMEM((TK, 2 * TQ), F32),
                        pltpu.VMEM((1, 2 * TQ), F32), pltpu.VMEM((1, 2 * TQ), F32),
                        pltpu.VMEM((V_ROWS, 2 * TQ), F32), pltpu.VMEM((V_ROWS, 2 * TQ), F32)],
        compiler_params=pltpu.CompilerParams(
            dimension_semantics=("arbitrary", "arbitrary"),
            vmem_limit_bytes=VMEM_LIMIT),
        name="fox_attention",
    )(ftab, q3, k3, vt3, faug, z3, go_row)


def _outproj_kernel(x_ref, y_ref, o_ref, mod_ref, wy_ref, wo_ref, out_ref):
    mixed = jnp.dot(y_ref[...], wy_ref[...], preferred_element_type=F32)
    mixed = mixed + jnp.dot(o_ref[...], wo_ref[...], preferred_element_type=F32)
    out_ref[...] = x_ref[...] + mod_ref[2, 0] * mixed


def _outproj_call(x2, y2, o2, mod4, wy, wo, seq):
    rows = x2.shape[0]
    tiles_per_batch = seq // TM_PROJ
    row_spec = lambda n: pl.BlockSpec((TM_PROJ, n), lambda i: (i, 0))
    return pl.pallas_call(
        _outproj_kernel,
        grid=(rows // TM_PROJ,),
        in_specs=[row_spec(D_MODEL), row_spec(D_SSM), row_spec(D_ATTN),
                  pl.BlockSpec((3, 1, 1, D_MODEL), lambda i: (0, i // tiles_per_batch, 0, 0)),
                  _const_spec(wy.shape), _const_spec(wo.shape)],
        out_specs=row_spec(D_MODEL),
        out_shape=jax.ShapeDtypeStruct((rows, D_MODEL), F32),
        compiler_params=pltpu.CompilerParams(dimension_semantics=("arbitrary",),
                                             vmem_limit_bytes=VMEM_LIMIT),
        name="out_proj",
    )(x2, y2, o2, mod4, wy, wo)


def _pad_cols(w, width):
    return jnp.pad(w, ((0, 0), (0, width - w.shape[1])))


def _layer(x, c, norm_gain, w_ada, b_ada, w_in, conv_w, conv_b, dt_bias, a_log, d_skip,
           ssm_norm_gain, q_norm_gain, k_norm_gain, forget_bias, attn_norm_gain, w_out):
    bsz, seq, _ = x.shape
    assert seq % TM_PROJ == 0 and TG == TM_PROJ and seq % TQ == 0 and seq % K_PREP_ROWS == 0
    assert KT_PER_QT == 2 and TQ % QT == 0 and TK % QT == 0
    x2 = x.reshape(bsz * seq, D_MODEL)

    mod4 = _ada_call(c, w_ada, b_ada).reshape(3, bsz, 1, D_MODEL)

    o_x = D_SSM
    o_dt = o_x + CONV_DIM
    o_q = o_dt + SSM_HEADS
    o_f = o_q + 4 * D_ATTN
    w_dt = w_in[:, o_dt:o_dt + SSM_HEADS]
    w_f3 = jnp.repeat(w_in[:, o_f:o_f + ATTN_HEADS], F_SPLIT, axis=1)
    weights = [w_in[:, :D_SSM], w_in[:, o_x:o_x + CONV_DIM],
               _pad_cols(jnp.concatenate([w_dt, w_dt, w_f3], axis=1), GATE_COLS),
               w_in[:, o_q:o_q + D_ATTN], w_in[:, o_q + D_ATTN:o_q + 2 * D_ATTN],
               w_in[:, o_q + 2 * D_ATTN:o_q + 3 * D_ATTN].T, w_in[:, o_q + 3 * D_ATTN:o_q + 4 * D_ATTN]]
    weights = [w.astype(BF16) for w in weights]
    gqk = jnp.stack([jnp.tile(q_norm_gain, ATTN_HEADS) * (LOG2E / math.sqrt(ATTN_HEAD_DIM)),
                     jnp.tile(k_norm_gain, ATTN_HEADS)]).astype(F32)
    a_neg = -jnp.exp(a_log.astype(F32)) * LOG2E
    f_lane = jnp.arange(F_SPLIT * ATTN_HEADS) % F_SPLIT
    plan_rows = [jnp.concatenate([dt_bias, dt_bias, jnp.repeat(forget_bias, F_SPLIT)]),
                 jnp.concatenate([jnp.zeros_like(a_neg), a_neg])]
    plan_rows += [jnp.concatenate([jnp.zeros((F_LANE0,), F32), (f_lane == t).astype(F32)])
                  for t in range(F_SPLIT)]
    plan = jnp.stack([jnp.pad(r.astype(F32), (0, GATE_COLS - r.shape[0])) for r in plan_rows])
    plan = jnp.pad(plan, ((0, 8 - plan.shape[0]), (0, 0)))
    y, q, k, vt, z_attn, faug, ftab = _proj_ssd_call(
        x2, mod4, norm_gain.reshape(1, D_MODEL), gqk, plan, _conv_shift_matrix(), conv_w,
        conv_b.reshape(1, CONV_DIM), jnp.repeat(d_skip, SSM_HEAD_DIM).reshape(1, D_SSM),
        ssm_norm_gain.reshape(1, D_SSM), weights, bsz, seq)
    faug = faug.reshape(bsz, seq, GATE_COLS)
    ftab = ftab.reshape(bsz, seq // TG, 1, GATE_COLS)
    to3 = lambda t: t.reshape(bsz, seq, t.shape[-1])

    ftab_flat = jnp.transpose(ftab[:, :, 0, F_LANE0:F_LANE0 + F_SPLIT * ATTN_HEADS:F_SPLIT],
                              (0, 2, 1)).reshape(-1)
    o = _attn_call(ftab_flat, to3(q), to3(k), vt, faug, to3(z_attn), attn_norm_gain.reshape(1, D_ATTN))

    out = _outproj_call(x2, y, o.reshape(bsz * seq, D_ATTN), mod4,
                        w_out[:D_SSM].astype(BF16), w_out[D_SSM:].astype(BF16), seq)
    return out.reshape(bsz, seq, D_MODEL)


def kernel(x, c, norm_gain, w_ada, b_ada, w_in, conv_w, conv_b, dt_bias, a_log, d_skip,
           ssm_norm_gain, q_norm_gain, k_norm_gain, forget_bias, attn_norm_gain, w_out):
    for layer in range(norm_gain.shape[0]):
        x = _layer(x, c, norm_gain[layer], w_ada[layer], b_ada[layer], w_in[layer],
                   conv_w[layer], conv_b[layer], dt_bias[layer], a_log[layer], d_skip[layer],
                   ssm_norm_gain[layer], q_norm_gain[layer], k_norm_gain[layer],
                   forget_bias[layer], attn_norm_gain[layer], w_out[layer])
    return x
```

```python
import functools
import math

import jax
import jax.numpy as jnp
from jax import lax
from jax.experimental import pallas as pl
from jax.experimental.pallas import tpu as pltpu

F32 = jnp.float32
BF16 = jnp.bfloat16

D_MODEL = 1024
D_SSM = 1024
D_ATTN = 1024
SSM_HEAD_DIM = 64
SSM_HEADS = 16
SSM_GROUPS = 2
SSM_STATE = 128
CONV_K = 4
CHUNK = 128
D_BC = SSM_GROUPS * SSM_STATE
CONV_DIM = D_SSM + 2 * D_BC
ATTN_HEAD_DIM = 64
ATTN_HEADS = 16
EPS = 1e-6
LANES = 128
HEAD_PAIRS = ATTN_HEADS // 2
GATE_COLS = LANES
F_LANE0 = 2 * SSM_HEADS
F_SPLIT = 3
LOG2E = math.log2(math.e)

TM_PROJ = 512
TG = 512
TQ = 1024
TK = TG
KT_PER_QT = TQ // TK
QT = 256
V_ROWS = LANES + 16
K_PREP_ROWS = 1024
NEG_BIG = -1e30
VMEM_LIMIT = 56 * 1024 * 1024


def _silu(v):
    return v * jax.nn.sigmoid(v)


def _const_spec(shape):
    nd = len(shape)
    return pl.BlockSpec(shape, lambda *_: (0,) * nd, pipeline_mode=pl.Buffered(1))


def _ada_kernel(c_ref, w_ref, b_ref, o_ref):
    c = c_ref[...]
    o_ref[0] = jnp.dot(_silu(c), w_ref[...], precision=lax.Precision.HIGHEST,
                       preferred_element_type=F32) + b_ref[0]


def _ada_call(c, w_ada, b_ada):
    bsz = c.shape[0]
    return pl.pallas_call(
        _ada_kernel,
        grid=(3,),
        in_specs=[pl.BlockSpec((bsz, D_MODEL), lambda j: (0, 0)),
                  pl.BlockSpec((D_MODEL, D_MODEL), lambda j: (0, j)),
                  pl.BlockSpec((1, 1, D_MODEL), lambda j: (j, 0, 0))],
        out_specs=pl.BlockSpec((1, bsz, D_MODEL), lambda j: (j, 0, 0)),
        out_shape=jax.ShapeDtypeStruct((3, bsz, D_MODEL), F32),
        name="adaln_mod",
    )(c, w_ada, b_ada.reshape(3, 1, D_MODEL))


CHUNKS_PER_TILE = TM_PROJ // CHUNK
PROJ_SLAB = 256


def _pair_rms(v, lo):
    sq = v * v
    s_lo = jnp.sum(jnp.where(lo, sq, 0.0), axis=-1, keepdims=True)
    s_hi = jnp.sum(jnp.where(lo, 0.0, sq), axis=-1, keepdims=True)
    return lax.rsqrt(jnp.where(lo, s_lo, s_hi) * (1.0 / ATTN_HEAD_DIM) + EPS)


def _conv_shift_matrix():
    t = jnp.arange(CHUNK)[:, None]
    r = jnp.arange(2 * CHUNK)[None, :]
    blocks = [(r == t + CHUNK - (CONV_K - 1) + k) for k in range(CONV_K - 1)]
    return jnp.concatenate(blocks, axis=0).astype(BF16)


def _gate_prefix(graw, plan_ref, carry_ref, nat_s, gt_s, faug_ref, ftab_ref):
    g = graw + plan_ref[0:1, :]
    soft = jnp.log1p(jnp.exp(-jnp.abs(g)))
    dt = jnp.maximum(g, 0.0) + soft
    logf = jnp.minimum(g, 0.0) - soft
    lane = lax.broadcasted_iota(jnp.int32, (1, LANES), 1)
    is_dt = lane < SSM_HEADS
    is_f = (lane >= F_LANE0) & (lane < F_LANE0 + F_SPLIT * ATTN_HEADS)
    src = jnp.where(is_dt, 0.0, jnp.where(lane < F_LANE0, dt * plan_ref[1:2, :],
                                          jnp.where(is_f, logf, 0.0)))
    r_i = lax.broadcasted_iota(jnp.int32, (CHUNK, CHUNK), 0)
    c_i = lax.broadcasted_iota(jnp.int32, (CHUNK, CHUNK), 1)
    tri = (r_i >= c_i).astype(F32)
    carry = carry_ref[...]
    chunks = []
    for ch in range(CHUNKS_PER_TILE):
        rows = slice(ch * CHUNK, (ch + 1) * CHUNK)
        pref = jnp.dot(tri, src[rows], precision=lax.Precision.HIGHEST, preferred_element_type=F32)
        res = jnp.where(is_dt, dt[rows], pref + jnp.where(is_f, carry, 0.0))
        carry = res[CHUNK - 1:CHUNK, :]
        nat_s[rows, :] = res
        gt_s[:, rows] = res.T[0:2 * SSM_HEADS]
        chunks.append(res)
    carry_ref[...] = carry

    fcum = jnp.concatenate(chunks, axis=0)
    first = fcum[0:1, :]
    ftab_ref[0] = first
    d = (first - fcum) * LOG2E
    hi = d.astype(BF16).astype(F32)
    r1 = d - hi
    mid = r1.astype(BF16).astype(F32)
    lo = (r1 - mid).astype(BF16).astype(F32)
    faug_ref[...] = (plan_ref[2:3, :] * hi + plan_ref[3:4, :] * mid + plan_ref[4:5, :] * lo).astype(BF16)


def _ssd_chunk(sub, prev_tail, ubuf, z_s, nat_s, gt_s, shift_ref, cw_ref, cb_ref, dskip_ref, gain_ref,
               y_ref, state, ybuf, fillers):
    fillers = list(fillers)
    lane = lax.broadcasted_iota(jnp.int32, (1, LANES), 1)
    lo = lane < SSM_HEAD_DIM
    row_i = lax.broadcasted_iota(jnp.int32, (CHUNK, CHUNK), 0)
    col_i = lax.broadcasted_iota(jnp.int32, (CHUNK, CHUNK), 1)
    causal = row_i >= col_i
    lo_rows = row_i < SSM_HEAD_DIM
    contract_last = (((1,), (1,)), ((), ()))
    contract_first = (((0,), (0,)), ((), ()))
    width = D_SSM // SSM_GROUPS
    rows = slice(sub * CHUNK, (sub + 1) * CHUNK)

    fillers.pop(0)()
    prev = prev_tail if sub == 0 else ubuf[(sub - 1) * CHUNK:sub * CHUNK, :]
    window = jnp.concatenate([prev, ubuf[rows, :]], axis=0)
    shifted = jnp.dot(shift_ref[...], window, preferred_element_type=F32)
    conv = cw_ref[CONV_K - 1:CONV_K, :] * window[CHUNK:, :].astype(F32) + cb_ref[...]
    for k in range(CONV_K - 1):
        conv = conv + cw_ref[k:k + 1, :] * shifted[k * CHUNK:(k + 1) * CHUNK, :]
    xbc = _silu(conv)

    nat = nat_s[rows, :]
    gt = gt_s[:, rows]
    for grp in range(SSM_GROUPS):
        b_g = xbc[:, D_SSM + grp * SSM_STATE:D_SSM + (grp + 1) * SSM_STATE].astype(BF16)
        c_g = xbc[:, D_SSM + D_BC + grp * SSM_STATE:D_SSM + D_BC + (grp + 1) * SSM_STATE].astype(BF16)
        cb = lax.dot_general(c_g, b_g, contract_last, preferred_element_type=F32)
        for pr in range(SSM_HEADS // SSM_GROUPS // 2):
            pair = grp * (SSM_HEADS // SSM_GROUPS // 2) + pr
            xp = xbc[:, pair * LANES:(pair + 1) * LANES]
            masks, ecol, dte, cdec = [], [], [], []
            for hd in (2 * pair, 2 * pair + 1):
                cum_col = nat[:, SSM_HEADS + hd:SSM_HEADS + hd + 1]
                dt_col = nat[:, hd:hd + 1]
                cum_row = gt[SSM_HEADS + hd:SSM_HEADS + hd + 1, :]
                dt_row = gt[hd:hd + 1, :]
                cum_last = cum_col[CHUNK - 1:CHUNK, :]
                decay = jnp.where(causal, jnp.exp2(cum_col - cum_row), 0.0)
                masks.append((cb * decay * dt_row).astype(BF16))
                ecol.append(jnp.exp2(cum_col))
                dte.append(dt_col * jnp.exp2(cum_last - cum_col))
                cdec.append(jnp.exp2(cum_last))
            yd = jnp.dot(jnp.concatenate(masks, axis=0), xp.astype(BF16),
                         preferred_element_type=F32)
            y_diag = jnp.where(lo, yd[:CHUNK], yd[CHUNK:])
            st = state[pair]
            y_off = lax.dot_general(c_g, st.astype(BF16), contract_last, preferred_element_type=F32)
            y_off = y_off * jnp.where(lo, ecol[0], ecol[1])
            xs = (xp * jnp.where(lo, dte[0], dte[1])).astype(BF16)
            upd = lax.dot_general(xs, b_g, contract_first, preferred_element_type=F32)
            state[pair] = st * jnp.where(lo_rows, cdec[0], cdec[1]) + upd
            ybuf[rows, pair * LANES:(pair + 1) * LANES] = (
                y_diag + y_off + xp * dskip_ref[:, pair * LANES:(pair + 1) * LANES])
            if pair % 2 == 1 and fillers:
                fillers.pop(0)()

    while fillers:
        fillers.pop(0)()
    gated = ybuf[rows, :] * _silu(z_s[rows, :].astype(F32))
    for grp in range(SSM_GROUPS):
        yg = gated[:, grp * width:(grp + 1) * width]
        ms = jnp.mean(yg * yg, axis=-1, keepdims=True)
        y_ref[rows, grp * width:(grp + 1) * width] = (
            yg * lax.rsqrt(ms + EPS) * gain_ref[:, grp * width:(grp + 1) * width]).astype(y_ref.dtype)


def _proj_ssd_kernel(x_ref, mod_ref, gain_ref, gqk_ref, plan_ref, shift_ref, cw_ref, cb_ref,
                     dskip_ref, ygain_ref, wz, wxbc, wg, wq, wk, wvt, wza,
                     oy, oq, ok, ovt, oza, ofaug, oftab,
                     carry_ref, ubuf2, z_s, nat_s, gt_s, state, ybuf, *, tiles_per_batch):
    @pl.when(pl.program_id(0) % tiles_per_batch == 0)
    def _():
        carry_ref[...] = jnp.zeros_like(carry_ref)
        state[...] = jnp.zeros_like(state)

    x = x_ref[...]
    ms = jnp.mean(x * x, axis=-1, keepdims=True)
    shift = mod_ref[0, 0]
    scale = mod_ref[1, 0]
    h = (x * lax.rsqrt(ms + EPS) * gain_ref[...]) * (1.0 + scale) + shift
    hb = h.astype(BF16)

    slot = pl.program_id(0) % 2
    ubuf = ubuf2.at[slot]
    ubuf[...] = jnp.dot(hb, wxbc[...], preferred_element_type=F32).astype(BF16)
    first_tile = pl.program_id(0) % tiles_per_batch == 0
    prev_tail = ubuf2[1 - slot, TM_PROJ - CHUNK:TM_PROJ, :]
    prev_tail = jnp.where(first_tile, jnp.zeros_like(prev_tail), prev_tail)
    _gate_prefix(jnp.dot(hb, wg[...], preferred_element_type=F32), plan_ref, carry_ref,
                 nat_s, gt_s, ofaug, oftab)

    lo = lax.broadcasted_iota(jnp.int32, (1, LANES), 1) < ATTN_HEAD_DIM

    def z_ssd(slab):
        z_s[:, slab] = jnp.dot(hb, wz[:, slab], preferred_element_type=F32).astype(BF16)

    def normed_heads(row, w_ref, o_ref, slab):
        full = jnp.dot(hb, w_ref[:, slab], preferred_element_type=F32)
        for g in range(PROJ_SLAB // LANES):
            cols = slice(slab.start + g * LANES, slab.start + (g + 1) * LANES)
            t = full[:, g * LANES:(g + 1) * LANES]
            o_ref[:, cols] = (t * _pair_rms(t, lo) * gqk_ref[row:row + 1, cols]).astype(o_ref.dtype)

    def v_transposed(slab):
        ovt[0, slab, :] = lax.dot_general(wvt[slab, :], hb, (((1,), (1,)), ((), ())),
                                          preferred_element_type=F32).astype(ovt.dtype)

    def z_attn(slab):
        oza[:, slab] = jnp.dot(hb, wza[:, slab], preferred_element_type=F32).astype(oza.dtype)

    slabs = [slice(s * PROJ_SLAB, (s + 1) * PROJ_SLAB) for s in range(D_ATTN // PROJ_SLAB)]
    pieces = ([functools.partial(z_ssd, s) for s in slabs]
              + [functools.partial(normed_heads, 0, wq, oq, s) for s in slabs]
              + [functools.partial(normed_heads, 1, wk, ok, s) for s in slabs]
              + [functools.partial(v_transposed, s) for s in slabs]
              + [functools.partial(z_attn, s) for s in slabs])
    per_chunk = len(pieces) // CHUNKS_PER_TILE
    for sub in range(CHUNKS_PER_TILE):
        _ssd_chunk(sub, prev_tail, ubuf, z_s, nat_s, gt_s, shift_ref, cw_ref, cb_ref, dskip_ref, ygain_ref,
                   oy, state, ybuf, pieces[sub * per_chunk:(sub + 1) * per_chunk])


def _proj_ssd_call(x2, mod4, gain, gqk, plan, shift, conv_w, conv_b, dskip_row, ygain_row, weights,
                   bsz, seq):
    rows = x2.shape[0]
    tiles_per_batch = seq // TM_PROJ
    assert CHUNKS_PER_TILE == 4
    row_spec = lambda n: pl.BlockSpec((TM_PROJ, n), lambda i: (i, 0))
    row_out = lambda n, dt: jax.ShapeDtypeStruct((rows, n), dt)
    consts = [gain, gqk, plan, shift, conv_w, conv_b, dskip_row, ygain_row] + list(weights)
    return pl.pallas_call(
        functools.partial(_proj_ssd_kernel, tiles_per_batch=tiles_per_batch),
        grid=(rows // TM_PROJ,),
        in_specs=[row_spec(D_MODEL),
                  pl.BlockSpec((3, 1, 1, D_MODEL), lambda i: (0, i // tiles_per_batch, 0, 0))]
                 + [_const_spec(c.shape) for c in consts],
        out_specs=[row_spec(D_SSM), row_spec(D_ATTN), row_spec(D_ATTN),
                   pl.BlockSpec((1, D_ATTN, TM_PROJ),
                                lambda i: (i // tiles_per_batch, 0, i % tiles_per_batch)),
                   row_spec(D_ATTN), row_spec(GATE_COLS),
                   pl.BlockSpec((1, 1, GATE_COLS), lambda i: (i, 0, 0))],
        out_shape=[row_out(D_SSM, BF16), row_out(D_ATTN, BF16), row_out(D_ATTN, BF16),
                   jax.ShapeDtypeStruct((bsz, D_ATTN, seq), BF16), row_out(D_ATTN, BF16),
                   row_out(GATE_COLS, BF16),
                   jax.ShapeDtypeStruct((rows // TM_PROJ, 1, GATE_COLS), F32)],
        scratch_shapes=[pltpu.VMEM((1, GATE_COLS), F32),
                        pltpu.VMEM((2, TM_PROJ, CONV_DIM), BF16),
                        pltpu.VMEM((TM_PROJ, D_SSM), BF16),
                        pltpu.VMEM((TM_PROJ, GATE_COLS), F32),
                        pltpu.VMEM((2 * SSM_HEADS, TM_PROJ), F32),
                        pltpu.VMEM((SSM_HEADS // 2, LANES, SSM_STATE), F32),
                        pltpu.VMEM((TM_PROJ, D_SSM), F32)],
        compiler_params=pltpu.CompilerParams(dimension_semantics=("arbitrary",),
                                             vmem_limit_bytes=VMEM_LIMIT),
        name="proj_ssd",
    )(x2, mod4, *consts)


def _attn_kernel(ftab_ref, q_ref, k_ref, vt_ref, faug_ref, z_ref, go_ref,
                 o_ref, kaug_ref, vaug_ref, qs_ref, s0_ref, s1_ref, cmax0_ref, cmax1_ref,
                 acc0_ref, acc1_ref, *, seq):
    b = pl.program_id(0)
    hp = pl.program_id(1)
    nblk = seq // TK
    nq = seq // TQ
    n_qt = 2 * TQ // QT
    s_refs, cmax_refs, acc_refs = (s0_ref, s1_ref), (cmax0_ref, cmax1_ref), (acc0_ref, acc1_ref)
    lane = lax.broadcasted_iota(jnp.int32, (1, LANES), 1)
    lo = lane < ATTN_HEAD_DIM
    contract_last = (((1,), (1,)), ((), ()))

    def prep(i, carry):
        rows = pl.ds(pl.multiple_of(i * K_PREP_ROWS, K_PREP_ROWS), K_PREP_ROWS)
        kaug_ref[rows, 0:LANES] = k_ref[0, rows, :]
        kaug_ref[rows, LANES:2 * LANES] = faug_ref[0, rows, :]
        vaug_ref[0:LANES, rows] = vt_ref[0, :, rows]
        return carry
    lax.fori_loop(0, seq // K_PREP_ROWS, prep, 0)
    vaug_ref[LANES:V_ROWS, :] = jnp.ones((V_ROWS - LANES, seq), BF16)

    def build_qs(qi, qslot):
        rows = pl.ds(pl.multiple_of(qi * TQ, TQ), TQ)
        qn = q_ref[0, rows, :]
        for half in range(2):
            f0 = F_LANE0 + F_SPLIT * (2 * hp + half)
            ones = jnp.where((lane >= f0) & (lane < f0 + F_SPLIT), 1.0, 0.0)
            keep = lo if half == 0 else jnp.logical_not(lo)
            qs_ref[qslot, half * TQ:(half + 1) * TQ, 0:LANES] = jnp.where(keep, qn, jnp.zeros_like(qn))
            qs_ref[qslot, half * TQ:(half + 1) * TQ, LANES:2 * LANES] = (
                jnp.broadcast_to(ones, (TQ, LANES)).astype(BF16))

    tab0 = (b * ATTN_HEADS + 2 * hp) * nblk

    def tile_offset(qi, j, qt):
        tab = tab0 + (qt * QT // TQ) * nblk
        base = jnp.full((1, QT), ftab_ref[tab + qi * KT_PER_QT])
        return (base - jnp.full((1, QT), ftab_ref[tab + j])) * LOG2E

    def fully_masked(qt, diag):
        return diag is not None and (qt * QT) % TQ + QT - 1 < diag * TK

    def a_tile(qi, j, qslot, slot, diag, qt):
        start = pl.multiple_of(j * TK, TK)
        cols = slice(qt * QT, (qt + 1) * QT)
        st = lax.dot_general(kaug_ref[pl.ds(start, TK), :], qs_ref[qslot, cols, :], contract_last,
                             preferred_element_type=F32)
        q_first = (qt * QT) % TQ
        if diag is not None and q_first < diag * TK + TK - 1:
            k_i = lax.broadcasted_iota(jnp.int32, (TK, QT), 0) + diag * TK
            q_i = lax.broadcasted_iota(jnp.int32, (TK, QT), 1) + q_first
            st = jnp.where(k_i <= q_i, st, NEG_BIG)
        s_refs[slot][:, cols] = st
        cmax_refs[slot][:, cols] = jnp.max(st, axis=0, keepdims=True)

    def bc_tile(qi, j, slot, acc_ref, first, qt, m):
        start = pl.multiple_of(j * TK, TK)
        cols = slice(qt * QT, (qt + 1) * QT)
        c = tile_offset(qi, j, qt)
        m_here = cmax_refs[slot][:, cols] + c
        m_new = m_here if first else jnp.maximum(m, m_here)
        p = jnp.exp2(s_refs[slot][:, cols] - (m_new - c)).astype(BF16)
        pv = jnp.dot(vaug_ref[:, pl.ds(start, TK)], p, preferred_element_type=F32)
        if first:
            acc_ref[:, cols] = pv
        else:
            acc_ref[:, cols] = jnp.exp2(m - m_new) * acc_ref[:, cols] + pv
        return m_new

    def half_region(a_step, bc_step, ms):
        ms = list(ms)
        for qt in range(n_qt):
            if a_step is not None and not fully_masked(qt, a_step[4]):
                qi, j, qslot, slot, diag = a_step
                a_tile(qi, j, qslot, slot, diag, qt)
            if bc_step is not None and not fully_masked(qt, bc_step[4]):
                qi, j, slot, acc_ref, _, first = bc_step
                ms[qt] = bc_tile(qi, j, slot, acc_ref, first, qt, ms[qt])
        return tuple(ms)

    def finalize(qi, acc_ref):
        rows = pl.ds(pl.multiple_of(qi * TQ, TQ), TQ)
        heads = []
        for half in range(2):
            ch = slice(half * ATTN_HEAD_DIM, (half + 1) * ATTN_HEAD_DIM)
            qs = slice(half * TQ, (half + 1) * TQ)
            o_h = acc_ref[ch, qs] * (1.0 / acc_ref[LANES:LANES + 1, qs])
            ms = jnp.mean(o_h * o_h, axis=0, keepdims=True)
            heads.append(o_h * lax.rsqrt(ms + EPS))
        o = jnp.concatenate(heads, axis=0).T * go_ref[...]
        o_ref[0, rows, :] = (o * _silu(z_ref[0, rows, :].astype(F32))).astype(o_ref.dtype)

    last_d = KT_PER_QT - 1

    def query_tile(qi, par, ms):
        acc, acc_prev = acc_refs[par], acc_refs[1 - par]
        n_off = qi * KT_PER_QT
        build_qs(qi, par)
        ms = half_region((qi, 0, par, 0, None), (qi - 1, n_off - 1, 1, acc_prev, last_d, False), ms)
        ms = half_region((qi, 1, par, 1, None), (qi, 0, 0, acc, None, True), ms)
        finalize(qi - 1, acc_prev)

        def pair(i, ms):
            ms = half_region((qi, 2 * i, par, 0, None), (qi, 2 * i - 1, 1, acc, None, False), ms)
            return half_region((qi, 2 * i + 1, par, 1, None), (qi, 2 * i, 0, acc, None, False), ms)
        def two_pairs(h, ms):
            return pair(2 * h + 2, pair(2 * h + 1, ms))
        ms = lax.fori_loop(0, (qi - 1) // 2, two_pairs, ms)
        ms = lax.cond((qi - 1) % 2 == 1, lambda v: pair(qi - 1, v), lambda v: v, ms)
        ms = half_region((qi, n_off, par, 0, 0), (qi, n_off - 1, 1, acc, None, False), ms)
        return half_region((qi, n_off + 1, par, 1, 1), (qi, n_off, 0, acc, 0, False), ms)

    ms = tuple(jnp.full((1, QT), NEG_BIG, F32) for _ in range(n_qt))
    build_qs(0, 0)
    ms = half_region((0, 0, 0, 0, 0), None, ms)
    ms = half_region((0, 1, 0, 1, 1), (0, 0, 0, acc0_ref, 0, True), ms)

    def two_tiles(u, ms):
        qi = 2 * u + 1
        ms = query_tile(qi, 1, ms)
        return lax.cond(qi + 1 < nq, lambda v: query_tile(qi + 1, 0, v), lambda v: v, ms)
    ms = lax.fori_loop(0, nq // 2, two_tiles, ms)
    acc_last = acc_refs[(nq - 1) % 2]
    half_region(None, (nq - 1, nq * KT_PER_QT - 1, 1, acc_last, last_d, False), ms)
    finalize(nq - 1, acc_last)


def _attn_call(ftab, q3, k3, vt3, faug, z3, go_row):
    bsz, seq, _ = q3.shape
    rows_spec = lambda imap: pl.BlockSpec((1, seq, LANES), imap)
    return pl.pallas_call(
        functools.partial(_attn_kernel, seq=seq),
        grid=(bsz, HEAD_PAIRS),
        in_specs=[pl.BlockSpec(memory_space=pltpu.SMEM),
                  rows_spec(lambda b, h: (b, 0, h)),
                  rows_spec(lambda b, h: (b, 0, h)),
                  pl.BlockSpec((1, LANES, seq), lambda b, h: (b, h, 0)),
                  rows_spec(lambda b, h: (b, 0, 0)),
                  rows_spec(lambda b, h: (b, 0, h)),
                  pl.BlockSpec((1, LANES), lambda b, h: (0, h))],
        out_specs=rows_spec(lambda b, h: (b, 0, h)),
        out_shape=jax.ShapeDtypeStruct((bsz, seq, D_ATTN), BF16),
        scratch_shapes=[pltpu.VMEM((seq, 2 * LANES), BF16),
                        pltpu.VMEM((V_ROWS, seq), BF16),
                        pltpu.VMEM((2, 2 * TQ, 2 * LANES), BF16),
                        pltpu.VMEM((TK, 2 * TQ), F32), pltpu.VMEM((TK, 2 * TQ), F32),
                        pltpu.VMEM((1, 2 * TQ), F32), pltpu.VMEM((1, 2 * TQ), F32),
                        pltpu.VMEM((V_ROWS, 2 * TQ), F32), pltpu.VMEM((V_ROWS, 2 * TQ), F32)],
        compiler_params=pltpu.CompilerParams(
            dimension_semantics=("arbitrary", "arbitrary"),
            vmem_limit_bytes=VMEM_LIMIT),
        name="fox_attention",
    )(ftab, q3, k3, vt3, faug, z3, go_row)


def _outproj_kernel(x_ref, y_ref, o_ref, mod_ref, wy_ref, wo_ref, out_ref):
    mixed = jnp.dot(y_ref[...], wy_ref[...], preferred_element_type=F32)
    mixed = mixed + jnp.dot(o_ref[...], wo_ref[...], preferred_element_type=F32)
    out_ref[...] = x_ref[...] + mod_ref[2, 0] * mixed


def _outproj_call(x2, y2, o2, mod4, wy, wo, seq):
    rows = x2.shape[0]
    tiles_per_batch = seq // TM_PROJ
    row_spec = lambda n: pl.BlockSpec((TM_PROJ, n), lambda i: (i, 0))
    return pl.pallas_call(
        _outproj_kernel,
        grid=(rows // TM_PROJ,),
        in_specs=[row_spec(D_MODEL), row_spec(D_SSM), row_spec(D_ATTN),
                  pl.BlockSpec((3, 1, 1, D_MODEL), lambda i: (0, i // tiles_per_batch, 0, 0)),
                  _const_spec(wy.shape), _const_spec(wo.shape)],
        out_specs=row_spec(D_MODEL),
        out_shape=jax.ShapeDtypeStruct((rows, D_MODEL), F32),
        compiler_params=pltpu.CompilerParams(dimension_semantics=("arbitrary",),
                                             vmem_limit_bytes=VMEM_LIMIT),
        name="out_proj",
    )(x2, y2, o2, mod4, wy, wo)


def _pad_cols(w, width):
    return jnp.pad(w, ((0, 0), (0, width - w.shape[1])))


def _layer(x, c, norm_gain, w_ada, b_ada, w_in, conv_w, conv_b, dt_bias, a_log, d_skip,
           ssm_norm_gain, q_norm_gain, k_norm_gain, forget_bias, attn_norm_gain, w_out):
    bsz, seq, _ = x.shape
    assert seq % TM_PROJ == 0 and TG == TM_PROJ and seq % TQ == 0 and seq % K_PREP_ROWS == 0
    assert KT_PER_QT == 2 and TQ % QT == 0 and TK % QT == 0
    x2 = x.reshape(bsz * seq, D_MODEL)

    mod4 = _ada_call(c, w_ada, b_ada).reshape(3, bsz, 1, D_MODEL)

    o_x = D_SSM
    o_dt = o_x + CONV_DIM
    o_q = o_dt + SSM_HEADS
    o_f = o_q + 4 * D_ATTN
    w_dt = w_in[:, o_dt:o_dt + SSM_HEADS]
    w_f3 = jnp.repeat(w_in[:, o_f:o_f + ATTN_HEADS], F_SPLIT, axis=1)
    weights = [w_in[:, :D_SSM], w_in[:, o_x:o_x + CONV_DIM],
               _pad_cols(jnp.concatenate([w_dt, w_dt, w_f3], axis=1), GATE_COLS),
               w_in[:, o_q:o_q + D_ATTN], w_in[:, o_q + D_ATTN:o_q + 2 * D_ATTN],
               w_in[:, o_q + 2 * D_ATTN:o_q + 3 * D_ATTN].T, w_in[:, o_q + 3 * D_ATTN:o_q + 4 * D_ATTN]]
    weights = [w.astype(BF16) for w in weights]
    gqk = jnp.stack([jnp.tile(q_norm_gain, ATTN_HEADS) * (LOG2E / math.sqrt(ATTN_HEAD_DIM)),
                     jnp.tile(k_norm_gain, ATTN_HEADS)]).astype(F32)
    a_neg = -jnp.exp(a_log.astype(F32)) * LOG2E
    f_lane = jnp.arange(F_SPLIT * ATTN_HEADS) % F_SPLIT
    plan_rows = [jnp.concatenate([dt_bias, dt_bias, jnp.repeat(forget_bias, F_SPLIT)]),
                 jnp.concatenate([jnp.zeros_like(a_neg), a_neg])]
    plan_rows += [jnp.concatenate([jnp.zeros((F_LANE0,), F32), (f_lane == t).astype(F32)])
                  for t in range(F_SPLIT)]
    plan = jnp.stack([jnp.pad(r.astype(F32), (0, GATE_COLS - r.shape[0])) for r in plan_rows])
    plan = jnp.pad(plan, ((0, 8 - plan.shape[0]), (0, 0)))
    y, q, k, vt, z_attn, faug, ftab = _proj_ssd_call(
        x2, mod4, norm_gain.reshape(1, D_MODEL), gqk, plan, _conv_shift_matrix(), conv_w,
        conv_b.reshape(1, CONV_DIM), jnp.repeat(d_skip, SSM_HEAD_DIM).reshape(1, D_SSM),
        ssm_norm_gain.reshape(1, D_SSM), weights, bsz, seq)
    faug = faug.reshape(bsz, seq, GATE_COLS)
    ftab = ftab.reshape(bsz, seq // TG, 1, GATE_COLS)
    to3 = lambda t: t.reshape(bsz, seq, t.shape[-1])

    ftab_flat = jnp.transpose(ftab[:, :, 0, F_LANE0:F_LANE0 + F_SPLIT * ATTN_HEADS:F_SPLIT],
                              (0, 2, 1)).reshape(-1)
    o = _attn_call(ftab_flat, to3(q), to3(k), vt, faug, to3(z_attn), attn_norm_gain.reshape(1, D_ATTN))

    out = _outproj_call(x2, y, o.reshape(bsz * seq, D_ATTN), mod4,
                        w_out[:D_SSM].astype(BF16), w_out[D_SSM:].astype(BF16), seq)
    return out.reshape(bsz, seq, D_MODEL)


def kernel(x, c, norm_gain, w_ada, b_ada, w_in, conv_w, conv_b, dt_bias, a_log, d_skip,
           ssm_norm_gain, q_norm_gain, k_norm_gain, forget_bias, attn_norm_gain, w_out):
    for layer in range(norm_gain.shape[0]):
        x = _layer(x, c, norm_gain[layer], w_ada[layer], b_ada[layer], w_in[layer],
                   conv_w[layer], conv_b[layer], dt_bias[layer], a_log[layer], d_skip[layer],
                   ssm_norm_gain[layer], q_norm_gain[layer], k_norm_gain[layer],
                   forget_bias[layer], attn_norm_gain[layer], w_out[layer])
    return x
```

```python
import functools
import math

import jax
import jax.numpy as jnp
from jax import lax
from jax.experimental import pallas as pl
from jax.experimental.pallas import tpu as pltpu

F32 = jnp.float32
BF16 = jnp.bfloat16

D_MODEL = 1024
D_SSM = 1024
D_ATTN = 1024
SSM_HEAD_DIM = 64
SSM_HEADS = 16
SSM_GROUPS = 2
SSM_STATE = 128
CONV_K = 4
CHUNK = 128
D_BC = SSM_GROUPS * SSM_STATE
CONV_DIM = D_SSM + 2 * D_BC
ATTN_HEAD_DIM = 64
ATTN_HEADS = 16
EPS = 1e-6
LANES = 128
HEAD_PAIRS = ATTN_HEADS // 2
GATE_COLS = LANES
F_LANE0 = 2 * SSM_HEADS
F_SPLIT = 3
LOG2E = math.log2(math.e)

TM_PROJ = 512
TG = 512
TQ = 1024
TK = TG
KT_PER_QT = TQ // TK
QT = 256
V_ROWS = LANES + 16
K_PREP_ROWS = 1024
NEG_BIG = -1e30
VMEM_LIMIT = 56 * 1024 * 1024


def _silu(v):
    return v * jax.nn.sigmoid(v)


def _const_spec(shape):
    nd = len(shape)
    return pl.BlockSpec(shape, lambda *_: (0,) * nd, pipeline_mode=pl.Buffered(1))


def _ada_kernel(c_ref, w_ref, b_ref, o_ref):
    c = c_ref[...]
    o_ref[0] = jnp.dot(_silu(c), w_ref[...], precision=lax.Precision.HIGHEST,
                       preferred_element_type=F32) + b_ref[0]


def _ada_call(c, w_ada, b_ada):
    bsz = c.shape[0]
    return pl.pallas_call(
        _ada_kernel,
        grid=(3,),
        in_specs=[pl.BlockSpec((bsz, D_MODEL), lambda j: (0, 0)),
                  pl.BlockSpec((D_MODEL, D_MODEL), lambda j: (0, j)),
                  pl.BlockSpec((1, 1, D_MODEL), lambda j: (j, 0, 0))],
        out_specs=pl.BlockSpec((1, bsz, D_MODEL), lambda j: (j, 0, 0)),
        out_shape=jax.ShapeDtypeStruct((3, bsz, D_MODEL), F32),
        name="adaln_mod",
    )(c, w_ada, b_ada.reshape(3, 1, D_MODEL))


CHUNKS_PER_TILE = TM_PROJ // CHUNK
PROJ_SLAB = 256


def _pair_rms(v, lo):
    sq = v * v
    s_lo = jnp.sum(jnp.where(lo, sq, 0.0), axis=-1, keepdims=True)
    s_hi = jnp.sum(jnp.where(lo, 0.0, sq), axis=-1, keepdims=True)
    return lax.rsqrt(jnp.where(lo, s_lo, s_hi) * (1.0 / ATTN_HEAD_DIM) + EPS)


def _conv_shift_matrix():
    t = jnp.arange(CHUNK)[:, None]
    r = jnp.arange(2 * CHUNK)[None, :]
    blocks = [(r == t + CHUNK - (CONV_K - 1) + k) for k in range(CONV_K - 1)]
    return jnp.concatenate(blocks, axis=0).astype(BF16)


def _gate_prefix(graw, plan_ref, carry_ref, nat_s, gt_s, faug_ref, ftab_ref):
    g = graw + plan_ref[0:1, :]
    soft = jnp.log1p(jnp.exp(-jnp.abs(g)))
    dt = jnp.maximum(g, 0.0) + soft
    logf = jnp.minimum(g, 0.0) - soft
    lane = lax.broadcasted_iota(jnp.int32, (1, LANES), 1)
    is_dt = lane < SSM_HEADS
    is_f = (lane >= F_LANE0) & (lane < F_LANE0 + F_SPLIT * ATTN_HEADS)
    src = jnp.where(is_dt, 0.0, jnp.where(lane < F_LANE0, dt * plan_ref[1:2, :],
                                          jnp.where(is_f, logf, 0.0)))
    r_i = lax.broadcasted_iota(jnp.int32, (CHUNK, CHUNK), 0)
    c_i = lax.broadcasted_iota(jnp.int32, (CHUNK, CHUNK), 1)
    tri = (r_i >= c_i).astype(BF16)
    s_hi = src.astype(BF16)
    rem = src - s_hi.astype(F32)
    s_mid = rem.astype(BF16)
    s_lo = (rem - s_mid.astype(F32)).astype(BF16)
    carry = carry_ref[...]
    chunks = []
    for ch in range(CHUNKS_PER_TILE):
        rows = slice(ch * CHUNK, (ch + 1) * CHUNK)
        terms = jnp.dot(tri, jnp.concatenate([s_hi[rows], s_mid[rows], s_lo[rows]], axis=1),
                        preferred_element_type=F32)
        pref = terms[:, 0:LANES] + terms[:, LANES:2 * LANES] + terms[:, 2 * LANES:3 * LANES]
        res = jnp.where(is_dt, dt[rows], pref + jnp.where(is_f, carry, 0.0))
        carry = res[CHUNK - 1:CHUNK, :]
        nat_s[rows, :] = res
        gt_s[:, rows] = res.T[0:2 * SSM_HEADS]
        chunks.append(res)
    carry_ref[...] = carry

    fcum = jnp.concatenate(chunks, axis=0)
    first = fcum[0:1, :]
    ftab_ref[0] = first
    d = (first - fcum) * LOG2E
    hi = d.astype(BF16).astype(F32)
    r1 = d - hi
    mid = r1.astype(BF16).astype(F32)
    lo = (r1 - mid).astype(BF16).astype(F32)
    faug_ref[...] = (plan_ref[2:3, :] * hi + plan_ref[3:4, :] * mid + plan_ref[4:5, :] * lo).astype(BF16)


def _ssd_chunk(sub, prev_tail, ubuf, z_s, nat_s, gt_s, shift_ref, cw_ref, cb_ref, dskip_ref, gain_ref,
               y_ref, state, ybuf, fillers):
    fillers = list(fillers)
    lane = lax.broadcasted_iota(jnp.int32, (1, LANES), 1)
    lo = lane < SSM_HEAD_DIM
    row_i = lax.broadcasted_iota(jnp.int32, (CHUNK, CHUNK), 0)
    col_i = lax.broadcasted_iota(jnp.int32, (CHUNK, CHUNK), 1)
    causal = row_i >= col_i
    lo_rows = row_i < SSM_HEAD_DIM
    contract_last = (((1,), (1,)), ((), ()))
    contract_first = (((0,), (0,)), ((), ()))
    width = D_SSM // SSM_GROUPS
    rows = slice(sub * CHUNK, (sub + 1) * CHUNK)

    fillers.pop(0)()
    prev = prev_tail if sub == 0 else ubuf[(sub - 1) * CHUNK:sub * CHUNK, :]
    window = jnp.concatenate([prev, ubuf[rows, :]], axis=0)
    shifted = jnp.dot(shift_ref[...], window, preferred_element_type=F32)
    conv = cw_ref[CONV_K - 1:CONV_K, :] * window[CHUNK:, :].astype(F32) + cb_ref[...]
    for k in range(CONV_K - 1):
        conv = conv + cw_ref[k:k + 1, :] * shifted[k * CHUNK:(k + 1) * CHUNK, :]
    xbc = _silu(conv)

    nat = nat_s[rows, :]
    gt = gt_s[:, rows]
    for grp in range(SSM_GROUPS):
        b_g = xbc[:, D_SSM + grp * SSM_STATE:D_SSM + (grp + 1) * SSM_STATE].astype(BF16)
        c_g = xbc[:, D_SSM + D_BC + grp * SSM_STATE:D_SSM + D_BC + (grp + 1) * SSM_STATE].astype(BF16)
        cb = lax.dot_general(c_g, b_g, contract_last, preferred_element_type=F32)
        for pr in range(SSM_HEADS // SSM_GROUPS // 2):
            pair = grp * (SSM_HEADS // SSM_GROUPS // 2) + pr
            xp = xbc[:, pair * LANES:(pair + 1) * LANES]
            masks, ecol, dte, cdec = [], [], [], []
            for hd in (2 * pair, 2 * pair + 1):
                cum_col = nat[:, SSM_HEADS + hd:SSM_HEADS + hd + 1]
                dt_col = nat[:, hd:hd + 1]
                cum_row = gt[SSM_HEADS + hd:SSM_HEADS + hd + 1, :]
                dt_row = gt[hd:hd + 1, :]
                cum_last = cum_col[CHUNK - 1:CHUNK, :]
                decay = jnp.where(causal, jnp.exp2(cum_col - cum_row), 0.0)
                masks.append((cb * decay * dt_row).astype(BF16))
                ecol.append(jnp.exp2(cum_col))
                dte.append(dt_col * jnp.exp2(cum_last - cum_col))
                cdec.append(jnp.exp2(cum_last))
            yd = jnp.dot(jnp.concatenate(masks, axis=0), xp.astype(BF16),
                         preferred_element_type=F32)
            y_diag = jnp.where(lo, yd[:CHUNK], yd[CHUNK:])
            st = state[pair]
            y_off = lax.dot_general(c_g, st.astype(BF16), contract_last, preferred_element_type=F32)
            y_off = y_off * jnp.where(lo, ecol[0], ecol[1])
            xs = (xp * jnp.where(lo, dte[0], dte[1])).astype(BF16)
            upd = lax.dot_general(xs, b_g, contract_first, preferred_element_type=F32)
            state[pair] = st * jnp.where(lo_rows, cdec[0], cdec[1]) + upd
            ybuf[rows, pair * LANES:(pair + 1) * LANES] = (
                y_diag + y_off + xp * dskip_ref[:, pair * LANES:(pair + 1) * LANES])
            if pair % 2 == 1 and fillers:
                fillers.pop(0)()

    while fillers:
        fillers.pop(0)()
    gated = ybuf[rows, :] * _silu(z_s[rows, :].astype(F32))
    for grp in range(SSM_GROUPS):
        yg = gated[:, grp * width:(grp + 1) * width]
        ms = jnp.mean(yg * yg, axis=-1, keepdims=True)
        y_ref[rows, grp * width:(grp + 1) * width] = (
            yg * lax.rsqrt(ms + EPS) * gain_ref[:, grp * width:(grp + 1) * width]).astype(y_ref.dtype)


def _proj_ssd_kernel(x_ref, mod_ref, gain_ref, gqk_ref, plan_ref, shift_ref, cw_ref, cb_ref,
                     dskip_ref, ygain_ref, wz, wxbc, wg, wq, wk, wvt, wza,
                     oy, oq, ok, ovt, oza, ofaug, oftab,
                     carry_ref, ubuf2, z_s, nat_s, gt_s, state, ybuf, *, tiles_per_batch):
    @pl.when(pl.program_id(0) % tiles_per_batch == 0)
    def _():
        carry_ref[...] = jnp.zeros_like(carry_ref)
        state[...] = jnp.zeros_like(state)

    x = x_ref[...]
    ms = jnp.mean(x * x, axis=-1, keepdims=True)
    shift = mod_ref[0, 0]
    scale = mod_ref[1, 0]
    h = (x * lax.rsqrt(ms + EPS) * gain_ref[...]) * (1.0 + scale) + shift
    hb = h.astype(BF16)

    slot = pl.program_id(0) % 2
    ubuf = ubuf2.at[slot]
    ubuf[...] = jnp.dot(hb, wxbc[...], preferred_element_type=F32).astype(BF16)
    first_tile = pl.program_id(0) % tiles_per_batch == 0
    prev_tail = ubuf2[1 - slot, TM_PROJ - CHUNK:TM_PROJ, :]
    prev_tail = jnp.where(first_tile, jnp.zeros_like(prev_tail), prev_tail)
    _gate_prefix(jnp.dot(hb, wg[...], preferred_element_type=F32), plan_ref, carry_ref,
                 nat_s, gt_s, ofaug, oftab)

    lo = lax.broadcasted_iota(jnp.int32, (1, LANES), 1) < ATTN_HEAD_DIM

    def z_ssd(slab):
        z_s[:, slab] = jnp.dot(hb, wz[:, slab], preferred_element_type=F32).astype(BF16)

    def normed_heads(row, w_ref, o_ref, slab):
        full = jnp.dot(hb, w_ref[:, slab], preferred_element_type=F32)
        for g in range(PROJ_SLAB // LANES):
            cols = slice(slab.start + g * LANES, slab.start + (g + 1) * LANES)
            t = full[:, g * LANES:(g + 1) * LANES]
            o_ref[:, cols] = (t * _pair_rms(t, lo) * gqk_ref[row:row + 1, cols]).astype(o_ref.dtype)

    def v_transposed(slab):
        ovt[0, slab, :] = lax.dot_general(wvt[slab, :], hb, (((1,), (1,)), ((), ())),
                                          preferred_element_type=F32).astype(ovt.dtype)

    def z_attn(slab):
        oza[:, slab] = jnp.dot(hb, wza[:, slab], preferred_element_type=F32).astype(oza.dtype)

    slabs = [slice(s * PROJ_SLAB, (s + 1) * PROJ_SLAB) for s in range(D_ATTN // PROJ_SLAB)]
    pieces = ([functools.partial(z_ssd, s) for s in slabs]
              + [functools.partial(normed_heads, 0, wq, oq, s) for s in slabs]
              + [functools.partial(normed_heads, 1, wk, ok, s) for s in slabs]
              + [functools.partial(v_transposed, s) for s in slabs]
              + [functools.partial(z_attn, s) for s in slabs])
    per_chunk = len(pieces) // CHUNKS_PER_TILE
    for sub in range(CHUNKS_PER_TILE):
        _ssd_chunk(sub, prev_tail, ubuf, z_s, nat_s, gt_s, shift_ref, cw_ref, cb_ref, dskip_ref, ygain_ref,
                   oy, state, ybuf, pieces[sub * per_chunk:(sub + 1) * per_chunk])


def _proj_ssd_call(x2, mod4, gain, gqk, plan, shift, conv_w, conv_b, dskip_row, ygain_row, weights,
                   bsz, seq):
    rows = x2.shape[0]
    tiles_per_batch = seq // TM_PROJ
    assert CHUNKS_PER_TILE == 4
    row_spec = lambda n: pl.BlockSpec((TM_PROJ, n), lambda i: (i, 0))
    row_out = lambda n, dt: jax.ShapeDtypeStruct((rows, n), dt)
    consts = [gain, gqk, plan, shift, conv_w, conv_b, dskip_row, ygain_row] + list(weights)
    return pl.pallas_call(
        functools.partial(_proj_ssd_kernel, tiles_per_batch=tiles_per_batch),
        grid=(rows // TM_PROJ,),
        in_specs=[row_spec(D_MODEL),
                  pl.BlockSpec((3, 1, 1, D_MODEL), lambda i: (0, i // tiles_per_batch, 0, 0))]
                 + [_const_spec(c.shape) for c in consts],
        out_specs=[row_spec(D_SSM), row_spec(D_ATTN), row_spec(D_ATTN),
                   pl.BlockSpec((1, D_ATTN, TM_PROJ),
                                lambda i: (i // tiles_per_batch, 0, i % tiles_per_batch)),
                   row_spec(D_ATTN), row_spec(GATE_COLS),
                   pl.BlockSpec((1, 1, GATE_COLS), lambda i: (i, 0, 0))],
        out_shape=[row_out(D_SSM, BF16), row_out(D_ATTN, BF16), row_out(D_ATTN, BF16),
                   jax.ShapeDtypeStruct((bsz, D_ATTN, seq), BF16), row_out(D_ATTN, BF16),
                   row_out(GATE_COLS, BF16),
                   jax.ShapeDtypeStruct((rows // TM_PROJ, 1, GATE_COLS), F32)],
        scratch_shapes=[pltpu.VMEM((1, GATE_COLS), F32),
                        pltpu.VMEM((2, TM_PROJ, CONV_DIM), BF16),
                        pltpu.VMEM((TM_PROJ, D_SSM), BF16),
                        pltpu.VMEM((TM_PROJ, GATE_COLS), F32),
                        pltpu.VMEM((2 * SSM_HEADS, TM_PROJ), F32),
                        pltpu.VMEM((SSM_HEADS // 2, LANES, SSM_STATE), F32),
                        pltpu.VMEM((TM_PROJ, D_SSM), F32)],
        compiler_params=pltpu.CompilerParams(dimension_semantics=("arbitrary",),
                                             vmem_limit_bytes=VMEM_LIMIT),
        name="proj_ssd",
    )(x2, mod4, *consts)


def _attn_kernel(ftab_ref, q_ref, k_ref, vt_ref, faug_ref, z_ref, go_ref,
                 o_ref, kaug_ref, vaug_ref, qs_ref, s0_ref, s1_ref, cmax0_ref, cmax1_ref,
                 acc0_ref, acc1_ref, *, seq):
    b = pl.program_id(0)
    hp = pl.program_id(1)
    nblk = seq // TK
    nq = seq // TQ
    n_qt = 2 * TQ // QT
    s_refs, cmax_refs, acc_refs = (s0_ref, s1_ref), (cmax0_ref, cmax1_ref), (acc0_ref, acc1_ref)
    lane = lax.broadcasted_iota(jnp.int32, (1, LANES), 1)
    lo = lane < ATTN_HEAD_DIM
    contract_last = (((1,), (1,)), ((), ()))

    def prep(i, carry):
        rows = pl.ds(pl.multiple_of(i * K_PREP_ROWS, K_PREP_ROWS), K_PREP_ROWS)
        kaug_ref[rows, 0:LANES] = k_ref[0, rows, :]
        kaug_ref[rows, LANES:2 * LANES] = faug_ref[0, rows, :]
        vaug_ref[0:LANES, rows] = vt_ref[0, :, rows]
        return carry
    lax.fori_loop(0, seq // K_PREP_ROWS, prep, 0)
    vaug_ref[LANES:V_ROWS, :] = jnp.ones((V_ROWS - LANES, seq), BF16)

    def build_qs(qi, qslot):
        rows = pl.ds(pl.multiple_of(qi * TQ, TQ), TQ)
        qn = q_ref[0, rows, :]
        for half in range(2):
            f0 = F_LANE0 + F_SPLIT * (2 * hp + half)
            ones = jnp.where((lane >= f0) & (lane < f0 + F_SPLIT), 1.0, 0.0)
            keep = lo if half == 0 else jnp.logical_not(lo)
            qs_ref[qslot, half * TQ:(half + 1) * TQ, 0:LANES] = jnp.where(keep, qn, jnp.zeros_like(qn))
            qs_ref[qslot, half * TQ:(half + 1) * TQ, LANES:2 * LANES] = (
                jnp.broadcast_to(ones, (TQ, LANES)).astype(BF16))

    tab0 = (b * ATTN_HEADS + 2 * hp) * nblk

    def tile_offset(qi, j, qt):
        tab = tab0 + (qt * QT // TQ) * nblk
        base = jnp.full((1, QT), ftab_ref[tab + qi * KT_PER_QT])
        return (base - jnp.full((1, QT), ftab_ref[tab + j])) * LOG2E

    def fully_masked(qt, diag):
        return diag is not None and (qt * QT) % TQ + QT - 1 < diag * TK

    def a_tile(qi, j, qslot, slot, diag, qt):
        start = pl.multiple_of(j * TK, TK)
        cols = slice(qt * QT, (qt + 1) * QT)
        st = lax.dot_general(kaug_ref[pl.ds(start, TK), :], qs_ref[qslot, cols, :], contract_last,
                             preferred_element_type=F32)
        q_first = (qt * QT) % TQ
        if diag is not None and q_first < diag * TK + TK - 1:
            k_i = lax.broadcasted_iota(jnp.int32, (TK, QT), 0) + diag * TK
            q_i = lax.broadcasted_iota(jnp.int32, (TK, QT), 1) + q_first
            st = jnp.where(k_i <= q_i, st, NEG_BIG)
        s_refs[slot][:, cols] = st
        cmax_refs[slot][:, cols] = jnp.max(st, axis=0, keepdims=True)

    def bc_tile(qi, j, slot, acc_ref, first, qt, m):
        start = pl.multiple_of(j * TK, TK)
        cols = slice(qt * QT, (qt + 1) * QT)
        c = tile_offset(qi, j, qt)
        m_here = cmax_refs[slot][:, cols] + c
        m_new = m_here if first else jnp.maximum(m, m_here)
        p = jnp.exp2(s_refs[slot][:, cols] - (m_new - c)).astype(BF16)
        pv = jnp.dot(vaug_ref[:, pl.ds(start, TK)], p, preferred_element_type=F32)
        if first:
            acc_ref[:, cols] = pv
        else:
            acc_ref[:, cols] = jnp.exp2(m - m_new) * acc_ref[:, cols] + pv
        return m_new

    def half_region(a_step, bc_step, ms):
        ms = list(ms)
        for qt in range(n_qt):
            if a_step is not None and not fully_masked(qt, a_step[4]):
                qi, j, qslot, slot, diag = a_step
                a_tile(qi, j, qslot, slot, diag, qt)
            if bc_step is not None and not fully_masked(qt, bc_step[4]):
                qi, j, slot, acc_ref, _, first = bc_step
                ms[qt] = bc_tile(qi, j, slot, acc_ref, first, qt, ms[qt])
        return tuple(ms)

    def finalize(qi, acc_ref):
        rows = pl.ds(pl.multiple_of(qi * TQ, TQ), TQ)
        heads = []
        for half in range(2):
            ch = slice(half * ATTN_HEAD_DIM, (half + 1) * ATTN_HEAD_DIM)
            qs = slice(half * TQ, (half + 1) * TQ)
            o_h = acc_ref[ch, qs] * (1.0 / acc_ref[LANES:LANES + 1, qs])
            ms = jnp.mean(o_h * o_h, axis=0, keepdims=True)
            heads.append(o_h * lax.rsqrt(ms + EPS))
        o = jnp.concatenate(heads, axis=0).T * go_ref[...]
        o_ref[0, rows, :] = (o * _silu(z_ref[0, rows, :].astype(F32))).astype(o_ref.dtype)

    last_d = KT_PER_QT - 1

    def query_tile(qi, par, ms):
        acc, acc_prev = acc_refs[par], acc_refs[1 - par]
        n_off = qi * KT_PER_QT
        build_qs(qi, par)
        ms = half_region((qi, 0, par, 0, None), (qi - 1, n_off - 1, 1, acc_prev, last_d, False), ms)
        ms = half_region((qi, 1, par, 1, None), (qi, 0, 0, acc, None, True), ms)
        finalize(qi - 1, acc_prev)

        def pair(i, ms):
            ms = half_region((qi, 2 * i, par, 0, None), (qi, 2 * i - 1, 1, acc, None, False), ms)
            return half_region((qi, 2 * i + 1, par, 1, None), (qi, 2 * i, 0, acc, None, False), ms)
        def two_pairs(h, ms):
            return pair(2 * h + 2, pair(2 * h + 1, ms))
        ms = lax.fori_loop(0, (qi - 1) // 2, two_pairs, ms)
        ms = lax.cond((qi - 1) % 2 == 1, lambda v: pair(qi - 1, v), lambda v: v, ms)
        ms = half_region((qi, n_off, par, 0, 0), (qi, n_off - 1, 1, acc, None, False), ms)
        return half_region((qi, n_off + 1, par, 1, 1), (qi, n_off, 0, acc, 0, False), ms)

    ms = tuple(jnp.full((1, QT), NEG_BIG, F32) for _ in range(n_qt))
    build_qs(0, 0)
    ms = half_region((0, 0, 0, 0, 0), None, ms)
    ms = half_region((0, 1, 0, 1, 1), (0, 0, 0, acc0_ref, 0, True), ms)

    def two_tiles(u, ms):
        qi = 2 * u + 1
        ms = query_tile(qi, 1, ms)
        return lax.cond(qi + 1 < nq, lambda v: query_tile(qi + 1, 0, v), lambda v: v, ms)
    ms = lax.fori_loop(0, nq // 2, two_tiles, ms)
    acc_last = acc_refs[(nq - 1) % 2]
    half_region(None, (nq - 1, nq * KT_PER_QT - 1, 1, acc_last, last_d, False), ms)
    finalize(nq - 1, acc_last)


def _attn_call(ftab, q3, k3, vt3, faug, z3, go_row):
    bsz, seq, _ = q3.shape
    rows_spec = lambda imap: pl.BlockSpec((1, seq, LANES), imap)
    return pl.pallas_call(
        functools.partial(_attn_kernel, seq=seq),
        grid=(bsz, HEAD_PAIRS),
        in_specs=[pl.BlockSpec(memory_space=pltpu.SMEM),
                  rows_spec(lambda b, h: (b, 0, h)),
                  rows_spec(lambda b, h: (b, 0, h)),
                  pl.BlockSpec((1, LANES, seq), lambda b, h: (b, h, 0)),
                  rows_spec(lambda b, h: (b, 0, 0)),
                  rows_spec(lambda b, h: (b, 0, h)),
                  pl.BlockSpec((1, LANES), lambda b, h: (0, h))],
        out_specs=rows_spec(lambda b, h: (b, 0, h)),
        out_shape=jax.ShapeDtypeStruct((bsz, seq, D_ATTN), BF16),
        scratch_shapes=[pltpu.VMEM((seq, 2 * LANES), BF16),
                        pltpu.VMEM((V_ROWS, seq), BF16),
                        pltpu.VMEM((2, 2 * TQ, 2 * LANES), BF16),
                        pltpu.VMEM((TK, 2 * TQ), F32), pltpu.VMEM((TK, 2 * TQ), F32),
                        pltpu.VMEM((1, 2 * TQ), F32), pltpu.VMEM((1, 2 * TQ), F32),
                        pltpu.VMEM((V_ROWS, 2 * TQ), F32), pltpu.VMEM((V_ROWS, 2 * TQ), F32)],
        compiler_params=pltpu.CompilerParams(
            dimension_semantics=("arbitrary", "arbitrary"),
            vmem_limit_bytes=VMEM_LIMIT),
        name="fox_attention",
    )(ftab, q3, k3, vt3, faug, z3, go_row)


def _outproj_kernel(x_ref, y_ref, o_ref, mod_ref, wy_ref, wo_ref, out_ref):
    mixed = jnp.dot(y_ref[...], wy_ref[...], preferred_element_type=F32)
    mixed = mixed + jnp.dot(o_ref[...], wo_ref[...], preferred_element_type=F32)
    out_ref[...] = x_ref[...] + mod_ref[2, 0] * mixed


def _outproj_call(x2, y2, o2, mod4, wy, wo, seq):
    rows = x2.shape[0]
    tiles_per_batch = seq // TM_PROJ
    row_spec = lambda n: pl.BlockSpec((TM_PROJ, n), lambda i: (i, 0))
    return pl.pallas_call(
        _outproj_kernel,
        grid=(rows // TM_PROJ,),
        in_specs=[row_spec(D_MODEL), row_spec(D_SSM), row_spec(D_ATTN),
                  pl.BlockSpec((3, 1, 1, D_MODEL), lambda i: (0, i // tiles_per_batch, 0, 0)),
                  _const_spec(wy.shape), _const_spec(wo.shape)],
        out_specs=row_spec(D_MODEL),
        out_shape=jax.ShapeDtypeStruct((rows, D_MODEL), F32),
        compiler_params=pltpu.CompilerParams(dimension_semantics=("arbitrary",),
                                             vmem_limit_bytes=VMEM_LIMIT),
        name="out_proj",
    )(x2, y2, o2, mod4, wy, wo)


def _pad_cols(w, width):
    return jnp.pad(w, ((0, 0), (0, width - w.shape[1])))


def _layer(x, c, norm_gain, w_ada, b_ada, w_in, conv_w, conv_b, dt_bias, a_log, d_skip,
           ssm_norm_gain, q_norm_gain, k_norm_gain, forget_bias, attn_norm_gain, w_out):
    bsz, seq, _ = x.shape
    assert seq % TM_PROJ == 0 and TG == TM_PROJ and seq % TQ == 0 and seq % K_PREP_ROWS == 0
    assert KT_PER_QT == 2 and TQ % QT == 0 and TK % QT == 0
    x2 = x.reshape(bsz * seq, D_MODEL)

    mod4 = _ada_call(c, w_ada, b_ada).reshape(3, bsz, 1, D_MODEL)

    o_x = D_SSM
    o_dt = o_x + CONV_DIM
    o_q = o_dt + SSM_HEADS
    o_f = o_q + 4 * D_ATTN
    w_dt = w_in[:, o_dt:o_dt + SSM_HEADS]
    w_f3 = jnp.repeat(w_in[:, o_f:o_f + ATTN_HEADS], F_SPLIT, axis=1)
    weights = [w_in[:, :D_SSM], w_in[:, o_x:o_x + CONV_DIM],
               _pad_cols(jnp.concatenate([w_dt, w_dt, w_f3], axis=1), GATE_COLS),
               w_in[:, o_q:o_q + D_ATTN], w_in[:, o_q + D_ATTN:o_q + 2 * D_ATTN],
               w_in[:, o_q + 2 * D_ATTN:o_q + 3 * D_ATTN].T, w_in[:, o_q + 3 * D_ATTN:o_q + 4 * D_ATTN]]
    weights = [w.astype(BF16) for w in weights]
    gqk = jnp.stack([jnp.tile(q_norm_gain, ATTN_HEADS) * (LOG2E / math.sqrt(ATTN_HEAD_DIM)),
                     jnp.tile(k_norm_gain, ATTN_HEADS)]).astype(F32)
    a_neg = -jnp.exp(a_log.astype(F32)) * LOG2E
    f_lane = jnp.arange(F_SPLIT * ATTN_HEADS) % F_SPLIT
    plan_rows = [jnp.concatenate([dt_bias, dt_bias, jnp.repeat(forget_bias, F_SPLIT)]),
                 jnp.concatenate([jnp.zeros_like(a_neg), a_neg])]
    plan_rows += [jnp.concatenate([jnp.zeros((F_LANE0,), F32), (f_lane == t).astype(F32)])
                  for t in range(F_SPLIT)]
    plan = jnp.stack([jnp.pad(r.astype(F32), (0, GATE_COLS - r.shape[0])) for r in plan_rows])
    plan = jnp.pad(plan, ((0, 8 - plan.shape[0]), (0, 0)))
    y, q, k, vt, z_attn, faug, ftab = _proj_ssd_call(
        x2, mod4, norm_gain.reshape(1, D_MODEL), gqk, plan, _conv_shift_matrix(), conv_w,
        conv_b.reshape(1, CONV_DIM), jnp.repeat(d_skip, SSM_HEAD_DIM).reshape(1, D_SSM),
        ssm_norm_gain.reshape(1, D_SSM), weights, bsz, seq)
    faug = faug.reshape(bsz, seq, GATE_COLS)
    ftab = ftab.reshape(bsz, seq // TG, 1, GATE_COLS)
    to3 = lambda t: t.reshape(bsz, seq, t.shape[-1])

    ftab_flat = jnp.transpose(ftab[:, :, 0, F_LANE0:F_LANE0 + F_SPLIT * ATTN_HEADS:F_SPLIT],
                              (0, 2, 1)).reshape(-1)
    o = _attn_call(ftab_flat, to3(q), to3(k), vt, faug, to3(z_attn), attn_norm_gain.reshape(1, D_ATTN))

    out = _outproj_call(x2, y, o.reshape(bsz * seq, D_ATTN), mod4,
                        w_out[:D_SSM].astype(BF16), w_out[D_SSM:].astype(BF16), seq)
    return out.reshape(bsz, seq, D_MODEL)


def kernel(x, c, norm_gain, w_ada, b_ada, w_in, conv_w, conv_b, dt_bias, a_log, d_skip,
           ssm_norm_gain, q_norm_gain, k_norm_gain, forget_bias, attn_norm_gain, w_out):
    for layer in range(norm_gain.shape[0]):
        x = _layer(x, c, norm_gain[layer], w_ada[layer], b_ada[layer], w_in[layer],
                   conv_w[layer], conv_b[layer], dt_bias[layer], a_log[layer], d_skip[layer],
                   ssm_norm_gain[layer], q_norm_gain[layer], k_norm_gain[layer],
                   forget_bias[layer], attn_norm_gain[layer], w_out[layer])
    return x
```
